```python
import math
import jax, jax.numpy as jnp
from jax import lax
import numpy as np

D_MODEL = 1024
BATCH = 2
SEQ = 8192
DEPTH = 2

N_EVEN = (DEPTH + 1) // 2
N_ODD = DEPTH // 2

SSD_HEADS = 16
SSD_HEAD_DIM = 64
SSD_INNER = SSD_HEADS * SSD_HEAD_DIM
SSD_GROUPS = 2
SSD_STATE = 128
SSD_CHUNK = 128
SSD_CONV_DIM = SSD_INNER + 2 * SSD_GROUPS * SSD_STATE
CONV_WIDTH = 4
LRU_HEADS = 16
LRU_WIDTH = D_MODEL
LRU_BLOCK = LRU_WIDTH // LRU_HEADS
LRU_C = 8.0
IN_SPLITS = (SSD_INNER,
             SSD_INNER + SSD_CONV_DIM,
             SSD_INNER + SSD_CONV_DIM + SSD_HEADS,
             SSD_INNER + SSD_CONV_DIM + SSD_HEADS + LRU_WIDTH)
IN_COLS = IN_SPLITS[-1] + LRU_WIDTH
MIX_WIDTH = SSD_INNER + LRU_WIDTH
DIFF_HEADS = 8
DIFF_HEAD_DIM = 64
ROPE_DIM = DIFF_HEAD_DIM // 4
ROPE_THETA = 500000.0
Q_BLOCK = 128
D_FF = 4 * D_MODEL
ALPHA = (2 * DEPTH) ** 0.25
BETA = (8 * DEPTH) ** -0.25
EPS = 1e-5

kernel_name = "hybrid_ssd_rglru_diffattn_deepnorm"


def layer_norm(x, g, b):
    xf = x.astype(jnp.float32)
    mu = jnp.mean(xf, axis=-1, keepdims=True)
    var = jnp.mean(jnp.square(xf - mu), axis=-1, keepdims=True)
    return ((xf - mu) * lax.rsqrt(var + EPS)).astype(x.dtype) * g + b


def rms_normalize(x):
    xf = x.astype(jnp.float32)
    return (xf * lax.rsqrt(jnp.mean(jnp.square(xf), axis=-1, keepdims=True) + EPS)).astype(x.dtype)


def causal_conv(x, w, b):
    y = lax.conv_general_dilated(x, w[:, None, :], window_strides=(1,),
                                 padding=[(CONV_WIDTH - 1, 0)],
                                 dimension_numbers=('NWC', 'WIO', 'NWC'),
                                 feature_group_count=x.shape[-1])
    return y + b


def ssd_chunked(x, dt, a, bmat, cmat):
    bsz, s, h, p = x.shape
    g, n = bmat.shape[-2], bmat.shape[-1]
    r = h // g
    nc, l = s // SSD_CHUNK, SSD_CHUNK
    xdt = (x * dt[..., None]).reshape(bsz, nc, l, g, r, p)
    adt = (dt * a).reshape(bsz, nc, l, g, r).transpose(0, 3, 4, 1, 2)
    bc = bmat.reshape(bsz, nc, l, g, n)
    cc = cmat.reshape(bsz, nc, l, g, n)
    a_cum = jnp.cumsum(adt, axis=-1)
    causal = jnp.tril(jnp.ones((l, l), dtype=bool))
    seg = a_cum[..., :, None] - a_cum[..., None, :]
    decay = jnp.exp(jnp.where(causal, seg, -jnp.inf))
    cb = jnp.einsum('bclgn,bcsgn->bgcls', cc, bc)
    y_diag = jnp.einsum('bgcls,bgrcls,bcsgrp->bclgrp', cb, decay, xdt)
    decay_states = jnp.exp(a_cum[..., -1:] - a_cum)
    states = jnp.einsum('bcsgn,bgrcs,bcsgrp->bcgrpn', bc, decay_states, xdt)
    chunk_decay = jnp.exp(a_cum[..., -1])

    def step(hs, inp):
        st, dec = inp
        return dec[..., None, None] * hs + st, hs

    h0 = jnp.zeros(states.shape[:1] + states.shape[2:], states.dtype)
    _, prev = lax.scan(step, h0, (jnp.moveaxis(states, 1, 0), jnp.moveaxis(chunk_decay, 3, 0)))
    prev = jnp.moveaxis(prev, 0, 1)
    y_off = jnp.einsum('bclgn,bcgrpn,bgrcl->bclgrp', cc, prev, jnp.exp(a_cum))
    return (y_diag + y_off).reshape(bsz, s, h, p)


def rg_lru(x, w_a, b_a, w_x, b_x, lam):
    bsz, s, _ = x.shape
    xb = x.reshape(bsz, s, LRU_HEADS, LRU_BLOCK)
    r = jax.nn.sigmoid(jnp.einsum('bshi,hij->bshj', xb, w_a) + b_a).reshape(bsz, s, LRU_WIDTH)
    i = jax.nn.sigmoid(jnp.einsum('bshi,hij->bshj', xb, w_x) + b_x).reshape(bsz, s, LRU_WIDTH)
    log_a = (-LRU_C * r * jax.nn.softplus(-lam)).astype(jnp.float32)
    a = jnp.exp(log_a)
    mult = jnp.sqrt(jnp.maximum(-jnp.expm1(2.0 * log_a), 0.0))
    u = mult * (i * x).astype(jnp.float32)

    def combine(c1, c2):
        a1, b1 = c1
        a2, b2 = c2
        return a1 * a2, a2 * b1 + b2

    _, hs = lax.associative_scan(combine, (a, u), axis=1)
    return hs.astype(x.dtype)


def ssd_lru_mixer(x, w_in, ssm_conv_w, ssm_conv_b, dt_bias, a_log, d_skip, ssm_norm_w,
                  lru_conv_w, lru_conv_b, lru_w_a, lru_b_a, lru_w_x, lru_b_x, lru_lambda, w_out):
    bsz, s, _ = x.shape
    proj = x @ w_in
    z, xbc, dt_raw, gate_lru, x_lru = jnp.split(proj, IN_SPLITS, axis=-1)
    xbc = jax.nn.silu(causal_conv(xbc, ssm_conv_w, ssm_conv_b))
    xs, bm, cm = jnp.split(xbc, (SSD_INNER, SSD_INNER + SSD_GROUPS * SSD_STATE), axis=-1)
    dt = jax.nn.softplus(dt_raw + dt_bias)
    a = -jnp.exp(a_log)
    xh = xs.reshape(bsz, s, SSD_HEADS, SSD_HEAD_DIM)
    y = ssd_chunked(xh, dt, a, bm.reshape(bsz, s, SSD_GROUPS, SSD_STATE),
                    cm.reshape(bsz, s, SSD_GROUPS, SSD_STATE))
    y = (y + d_skip[:, None] * xh).reshape(bsz, s, SSD_INNER) * jax.nn.silu(z)
    y_a = rms_normalize(y.reshape(bsz, s, SSD_GROUPS, SSD_INNER // SSD_GROUPS)).reshape(bsz, s, SSD_INNER) * ssm_norm_w
    xl = causal_conv(x_lru, lru_conv_w, lru_conv_b)
    y_b = rg_lru(xl, lru_w_a, lru_b_a, lru_w_x, lru_b_x, lru_lambda) * jax.nn.gelu(gate_lru)
    return jnp.concatenate([y_a, y_b], axis=-1) @ w_out


def partial_rope(t, cos, sin):
    half = ROPE_DIM // 2
    t1, t2, rest = t[..., :half], t[..., half:ROPE_DIM], t[..., ROPE_DIM:]
    return jnp.concatenate([t1 * cos - t2 * sin, t2 * cos + t1 * sin, rest], axis=-1)


def diff_attention(x, positions, w_qkv, lq1, lk1, lq2, lk2, subln_w, w_out, lambda_init):
    bsz, s, _ = x.shape
    q, k, v = jnp.split(x @ w_qkv, 3, axis=-1)
    q = q.reshape(bsz, s, DIFF_HEADS, 2, DIFF_HEAD_DIM)
    k = k.reshape(bsz, s, DIFF_HEADS, 2, DIFF_HEAD_DIM)
    v = v.reshape(bsz, s, DIFF_HEADS, 2 * DIFF_HEAD_DIM)
    inv_freq = ROPE_THETA ** (-jnp.arange(0, ROPE_DIM, 2, dtype=jnp.float32) / ROPE_DIM)
    ang = positions.astype(jnp.float32)[..., None] * inv_freq
    cos = jnp.cos(ang)[:, :, None, None, :].astype(x.dtype)
    sin = jnp.sin(ang)[:, :, None, None, :].astype(x.dtype)
    q = partial_rope(q, cos, sin) * (DIFF_HEAD_DIM ** -0.5)
    k = partial_rope(k, cos, sin)
    lam = (jnp.exp(jnp.sum(lq1.astype(jnp.float32) * lk1.astype(jnp.float32)))
           - jnp.exp(jnp.sum(lq2.astype(jnp.float32) * lk2.astype(jnp.float32))) + lambda_init)
    nb = s // Q_BLOCK
    qb = q.reshape(bsz, nb, Q_BLOCK, DIFF_HEADS, 2, DIFF_HEAD_DIM).transpose(1, 0, 2, 3, 4, 5)
    key_pos = jnp.arange(s)

    def attend(args):
        qblk, start = args
        sc = jnp.einsum('bqhcd,bkhcd->bhcqk', qblk, k).astype(jnp.float32)
        qpos = start + jnp.arange(Q_BLOCK)
        sc = jnp.where(key_pos[None, :] <= qpos[:, None], sc, -jnp.inf)
        p = jax.nn.softmax(sc, axis=-1)
        attn = p[:, :, 0] - lam * p[:, :, 1]
        return jnp.einsum('bhqk,bkhe->bqhe', attn.astype(v.dtype), v)

    o = lax.map(attend, (qb, jnp.arange(nb) * Q_BLOCK))
    o = o.transpose(1, 0, 2, 3, 4).reshape(bsz, s, DIFF_HEADS, 2 * DIFF_HEAD_DIM)
    o = rms_normalize(o) * subln_w * (1.0 - lambda_init)
    return o.reshape(bsz, s, DIFF_HEADS * 2 * DIFF_HEAD_DIM) @ w_out


def sq_relu_mlp(x, w1, w2):
    return jnp.square(jax.nn.relu(x @ w1)) @ w2


def setup_inputs(seed: int = 0) -> dict:
    key = jax.random.key(seed)
    ks = jax.random.split(key, 40)

    def nrm(k, shape, scale):
        return jax.random.normal(k, shape, jnp.float32) * scale

    x = jax.random.normal(ks[0], (BATCH, SEQ, D_MODEL), jnp.float32)
    positions = jnp.broadcast_to(jnp.arange(SEQ, dtype=jnp.int32), (BATCH, SEQ))
    dt0 = jnp.exp(jax.random.uniform(ks[4], (N_EVEN, SSD_HEADS), jnp.float32,
                                     math.log(1e-3), math.log(1e-1)))
    lru_p = jax.random.uniform(ks[14], (N_EVEN, LRU_WIDTH), jnp.float32, 0.9, 0.999) ** (1.0 / LRU_C)
    return {
        "x": x,
        "positions": positions,
        "ssm_w_in": nrm(ks[1], (N_EVEN, D_MODEL, IN_COLS), D_MODEL ** -0.5),
        "ssm_conv_w": nrm(ks[2], (N_EVEN, CONV_WIDTH, SSD_CONV_DIM), CONV_WIDTH ** -0.5),
        "ssm_conv_b": nrm(ks[3], (N_EVEN, SSD_CONV_DIM), 0.02),
        "ssm_dt_bias": dt0 + jnp.log(-jnp.expm1(-dt0)),
        "ssm_a_log": jnp.log(jax.random.uniform(ks[5], (N_EVEN, SSD_HEADS), jnp.float32, 1.0, 16.0)),
        "ssm_d": 1.0 + nrm(ks[6], (N_EVEN, SSD_HEADS), 0.02),
        "ssm_norm_w": 1.0 + nrm(ks[7], (N_EVEN, SSD_INNER), 0.02),
        "lru_conv_w": nrm(ks[8], (N_EVEN, CONV_WIDTH, LRU_WIDTH), CONV_WIDTH ** -0.5),
        "lru_conv_b": nrm(ks[9], (N_EVEN, LRU_WIDTH), 0.02),
        "lru_w_a": nrm(ks[10], (N_EVEN, LRU_HEADS, LRU_BLOCK, LRU_BLOCK), LRU_BLOCK ** -0.5),
        "lru_b_a": nrm(ks[11], (N_EVEN, LRU_HEADS, LRU_BLOCK), 0.02),
        "lru_w_x": nrm(ks[12], (N_EVEN, LRU_HEADS, LRU_BLOCK, LRU_BLOCK), LRU_BLOCK ** -0.5),
        "lru_b_x": nrm(ks[13], (N_EVEN, LRU_HEADS, LRU_BLOCK), 0.02),
        "lru_lambda": jnp.log(lru_p) - jnp.log1p(-lru_p),
        "mix_w_out": nrm(ks[15], (N_EVEN, MIX_WIDTH, D_MODEL), BETA * MIX_WIDTH ** -0.5),
        "attn_w_qkv": nrm(ks[16], (N_ODD, D_MODEL, 3 * DIFF_HEADS * 2 * DIFF_HEAD_DIM), D_MODEL ** -0.5),
        "attn_lq1": nrm(ks[17], (N_ODD, DIFF_HEAD_DIM), 0.1),
        "attn_lk1": nrm(ks[18], (N_ODD, DIFF_HEAD_DIM), 0.1),
        "attn_lq2": nrm(ks[19], (N_ODD, DIFF_HEAD_DIM), 0.1),
        "attn_lk2": nrm(ks[20], (N_ODD, DIFF_HEAD_DIM), 0.1),
        "attn_subln_w": 1.0 + nrm(ks[21], (N_ODD, 2 * DIFF_HEAD_DIM), 0.02),
        "attn_w_out": nrm(ks[22], (N_ODD, DIFF_HEADS * 2 * DIFF_HEAD_DIM, D_MODEL),
                          BETA * (DIFF_HEADS * 2 * DIFF_HEAD_DIM) ** -0.5),
        "ln1_g": 1.0 + nrm(ks[23], (DEPTH, D_MODEL), 0.02),
        "ln1_b": nrm(ks[24], (DEPTH, D_MODEL), 0.02),
        "ff_w1": nrm(ks[25], (DEPTH, D_MODEL, D_FF), D_MODEL ** -0.5),
        "ff_w2": nrm(ks[26], (DEPTH, D_FF, D_MODEL), BETA * D_FF ** -0.5),
        "ln2_g": 1.0 + nrm(ks[27], (DEPTH, D_MODEL), 0.02),
        "ln2_b": nrm(ks[28], (DEPTH, D_MODEL), 0.02),
    }


def reference(x, positions, ssm_w_in, ssm_conv_w, ssm_conv_b, ssm_dt_bias, ssm_a_log, ssm_d,
              ssm_norm_w, lru_conv_w, lru_conv_b, lru_w_a, lru_b_a, lru_w_x, lru_b_x, lru_lambda,
              mix_w_out, attn_w_qkv, attn_lq1, attn_lk1, attn_lq2, attn_lk2, attn_subln_w,
              attn_w_out, ln1_g, ln1_b, ff_w1, ff_w2, ln2_g, ln2_b):
    for layer in range(DEPTH):
        i = layer // 2
        if layer % 2 == 0:
            mix = ssd_lru_mixer(x, ssm_w_in[i], ssm_conv_w[i], ssm_conv_b[i], ssm_dt_bias[i],
                                ssm_a_log[i], ssm_d[i], ssm_norm_w[i], lru_conv_w[i], lru_conv_b[i],
                                lru_w_a[i], lru_b_a[i], lru_w_x[i], lru_b_x[i], lru_lambda[i],
                                mix_w_out[i])
        else:
            lambda_init = 0.8 - 0.6 * math.exp(-0.3 * layer)
            mix = diff_attention(x, positions, attn_w_qkv[i], attn_lq1[i], attn_lk1[i],
                                 attn_lq2[i], attn_lk2[i], attn_subln_w[i], attn_w_out[i],
                                 lambda_init)
        x = layer_norm(ALPHA * x + mix, ln1_g[layer], ln1_b[layer])
        x = layer_norm(ALPHA * x + sq_relu_mlp(x, ff_w1[layer], ff_w2[layer]), ln2_g[layer], ln2_b[layer])
    return x
```

```python
import functools
import math

import jax
import jax.numpy as jnp
from jax import lax
from jax.experimental import pallas as pl
from jax.experimental.pallas import tpu as pltpu

F32 = jnp.float32
BF16 = jnp.bfloat16

SUBLANES = 8
LANES = 128

D_MODEL = 1024
DEPTH = 2
SSD_HEADS = 16
SSD_HEAD_DIM = 64
SSD_INNER = SSD_HEADS * SSD_HEAD_DIM
SSD_GROUPS = 2
SSD_STATE = 128
SSD_CHUNK = 128
SSD_CONV_DIM = SSD_INNER + 2 * SSD_GROUPS * SSD_STATE
CONV_WIDTH = 4
LRU_HEADS = 16
LRU_WIDTH = D_MODEL
LRU_BLOCK = LRU_WIDTH // LRU_HEADS
LRU_C = 8.0
LRU_TILE = 256
DIFF_HEADS = 8
DIFF_HEAD_DIM = 64
DIFF_V_DIM = 2 * DIFF_HEAD_DIM
ROPE_DIM = DIFF_HEAD_DIM // 4
ROPE_THETA = 500000.0
D_FF = 4 * D_MODEL
ALPHA = (2 * DEPTH) ** 0.25
EPS = 1e-5

PROJ_COLS = 3 * D_MODEL + SSD_CONV_DIM + LANES

VMEM_LIMIT = 56 * 1024 * 1024


def _cparams(*sem):
    return pltpu.CompilerParams(dimension_semantics=sem, vmem_limit_bytes=VMEM_LIMIT)


def _layer_norm(v, g, b):
    mu = jnp.mean(v, axis=-1, keepdims=True)
    d = v - mu
    var = jnp.mean(d * d, axis=-1, keepdims=True)
    return d * lax.rsqrt(var + EPS) * g + b


def _dot(a, b):
    return jnp.dot(a, b, preferred_element_type=F32)


def _dot_nt(a, b):
    return lax.dot_general(a, b, (((1,), (1,)), ((), ())), preferred_element_type=F32)


def _proj_kernel(x_ref, w_ref, o_ref):
    o_ref[...] = _dot(x_ref[...].astype(BF16), w_ref[...])


def _proj(x2d, w, tm=256):
    t, d = x2d.shape
    n = w.shape[1]
    return pl.pallas_call(
        _proj_kernel,
        grid=(t // tm,),
        in_specs=[pl.BlockSpec((tm, d), lambda i: (i, 0)),
                  pl.BlockSpec((d, n), lambda i: (0, 0))],
        out_specs=pl.BlockSpec((tm, n), lambda i: (i, 0)),
        out_shape=jax.ShapeDtypeStruct((t, n), F32),
        compiler_params=_cparams("arbitrary"),
        name="in_proj",
    )(x2d, w)


def _causal_conv(x_ref, xpad_ref, cw_ref, cb_ref, rows):
    xpad_ref[SUBLANES:SUBLANES + rows, :] = x_ref[...]
    out = cb_ref[...]
    for k in range(CONV_WIDTH):
        start = SUBLANES - (CONV_WIDTH - 1) + k
        out = out + cw_ref[k:k + 1, :] * xpad_ref[start:start + rows, :]
    xpad_ref[0:SUBLANES, :] = xpad_ref[rows:rows + SUBLANES, :]
    return out


def _expand_heads(v, e):
    hi = v.astype(BF16)
    lo = (v - hi.astype(F32)).astype(BF16)
    return _dot(hi, e) + _dot(lo, e)


def _ssd_kernel(z_ref, xbc_ref, dt_ref, cw_ref, cb_ref, dtb_ref, alog_ref, dsk_ref, nw_ref, e_ref,
                o_ref, xpad_ref, st_ref):
    l = SSD_CHUNK
    gw = SSD_INNER // SSD_GROUPS
    c = pl.program_id(1)

    @pl.when(c == 0)
    def _():
        xpad_ref[0:SUBLANES, :] = jnp.zeros((SUBLANES, SSD_CONV_DIM), F32)
        st_ref[...] = jnp.zeros_like(st_ref)

    xbc = jax.nn.silu(_causal_conv(xbc_ref, xpad_ref, cw_ref, cb_ref, l))
    xs = xbc[:, :SSD_INNER]
    bm = xbc[:, SSD_INNER:SSD_INNER + SSD_GROUPS * SSD_STATE]
    cm = xbc[:, SSD_INNER + SSD_GROUPS * SSD_STATE:]

    dt = jax.nn.softplus(dt_ref[...] + dtb_ref[...])
    adt = dt * (-jnp.exp(alog_ref[...]))
    row = lax.broadcasted_iota(jnp.int32, (l, l), 0)
    col = lax.broadcasted_iota(jnp.int32, (l, l), 1)
    causal = row >= col
    a_cum = jnp.dot(causal.astype(F32), adt, precision=lax.Precision.HIGHEST,
                    preferred_element_type=F32)
    a_cum_t = a_cum.T
    a_last = a_cum[l - 1:l, :]
    ea = jnp.exp(a_cum)
    stacked = jnp.concatenate([dt, ea, dt * jnp.exp(a_last - a_cum)], axis=0)
    ex = _expand_heads(stacked, e_ref[...])
    dt_x, ea_x, ds_x = ex[:l], ex[l:2 * l], ex[2 * l:]
    xdt = xs * dt_x
    x2 = (xs * ds_x).astype(BF16)
    cd_x = ea_x[l - 1:l, :]

    lane = lax.broadcasted_iota(jnp.int32, (l, LANES), 1)
    lo_half = lane < SSD_HEAD_DIM
    y_parts = []
    for g in range(SSD_GROUPS):
        bg = bm[:, g * SSD_STATE:(g + 1) * SSD_STATE]
        cg = cm[:, g * SSD_STATE:(g + 1) * SSD_STATE].astype(BF16)
        cb = _dot_nt(cg, bg.astype(BF16))
        heads_per_group = SSD_HEADS // SSD_GROUPS
        diag = []
        for pair in range(heads_per_group // 2):
            blk = g * (heads_per_group // 2) + pair
            xp = xdt[:, blk * LANES:(blk + 1) * LANES]
            acc = None
            for half in range(2):
                h = 2 * blk + half
                seg = a_cum[:, h:h + 1] - a_cum_t[h:h + 1, :]
                dec = jnp.exp(jnp.where(causal, seg, -jnp.inf))
                m = (cb * dec).astype(BF16)
                keep = lo_half if half == 0 else jnp.logical_not(lo_half)
                part = _dot(m, jnp.where(keep, xp, 0.0).astype(BF16))
                acc = part if acc is None else acc + part
            diag.append(acc)
        y_diag = jnp.concatenate(diag, axis=1)
        prev = st_ref[g]
        y_off = _dot(cg, prev.astype(BF16)) * ea_x[:, g * gw:(g + 1) * gw]
        st_new = _dot(bg.T.astype(BF16), x2[:, g * gw:(g + 1) * gw])
        st_ref[g] = prev * cd_x[:, g * gw:(g + 1) * gw] + st_new
        y_parts.append(y_diag + y_off)

    y = jnp.concatenate(y_parts, axis=1) + dsk_ref[...] * xs
    y = y * jax.nn.silu(z_ref[...])
    normed = []
    for g in range(SSD_GROUPS):
        yg = y[:, g * gw:(g + 1) * gw]
        normed.append(yg * lax.rsqrt(jnp.mean(yg * yg, axis=-1, keepdims=True) + EPS))
    o_ref[...] = (jnp.concatenate(normed, axis=1) * nw_ref[...]).astype(o_ref.dtype)


def _ssd(proj, bsz, seq, cw, cb, dtb, alog, dsk, nw, e):
    l = SSD_CHUNK
    nc = seq // l
    full = lambda shape: pl.BlockSpec(shape, lambda b, c: (0,) * len(shape))
    xbc_blk = (3 * D_MODEL) // SSD_CONV_DIM
    dt_blk = (3 * D_MODEL + SSD_CONV_DIM) // LANES
    return pl.pallas_call(
        _ssd_kernel,
        grid=(bsz, nc),
        in_specs=[
            pl.BlockSpec((l, SSD_INNER), lambda b, c: (b * nc + c, 0)),
            pl.BlockSpec((l, SSD_CONV_DIM), lambda b, c: (b * nc + c, xbc_blk)),
            pl.BlockSpec((l, LANES), lambda b, c: (b * nc + c, dt_blk)),
            full((CONV_WIDTH, SSD_CONV_DIM)), full((1, SSD_CONV_DIM)),
            full((1, LANES)), full((1, LANES)),
            full((1, SSD_INNER)), full((1, SSD_INNER)),
            full((LANES, SSD_INNER)),
        ],
        out_specs=pl.BlockSpec((l, SSD_INNER), lambda b, c: (b * nc + c, 0)),
        out_shape=jax.ShapeDtypeStruct((bsz * seq, SSD_INNER), BF16),
        scratch_shapes=[pltpu.VMEM((l + SUBLANES, SSD_CONV_DIM), F32),
                        pltpu.VMEM((SSD_GROUPS, SSD_STATE, SSD_INNER // SSD_GROUPS), F32)],
        compiler_params=_cparams("arbitrary", "arbitrary"),
        name="ssd_mixer",
    )(proj, proj, proj, cw, cb, dtb, alog, dsk, nw, e)


def _lru_kernel(gate_ref, xl_ref, cw_ref, cb_ref, wa_ref, ba_ref, wx_ref, bx_ref, lam_ref,
                o_ref, xpad_ref, h_ref, *, rows):
    c = pl.program_id(1)

    @pl.when(c == 0)
    def _():
        xpad_ref[0:SUBLANES, :] = jnp.zeros((SUBLANES, LRU_WIDTH), F32)
        h_ref[...] = jnp.zeros_like(h_ref)

    xl = _causal_conv(xl_ref, xpad_ref, cw_ref, cb_ref, rows)
    xlb = xl.astype(BF16)
    n_tiles = LRU_WIDTH // LRU_TILE

    def gate(w_ref, b_ref):
        parts = [_dot(xlb[:, t * LRU_TILE:(t + 1) * LRU_TILE], w_ref[t]) for t in range(n_tiles)]
        return jax.nn.sigmoid(jnp.concatenate(parts, axis=1) + b_ref[...])

    r = gate(wa_ref, ba_ref)
    i = gate(wx_ref, bx_ref)
    log_a = -LRU_C * r * jax.nn.softplus(-lam_ref[...])
    a = jnp.exp(log_a)
    mult = jnp.sqrt(jnp.maximum(1.0 - jnp.exp(2.0 * log_a), 0.0))
    u = mult * (i * xl)

    rowi = lax.broadcasted_iota(jnp.int32, (rows, LRU_WIDTH), 0)
    k = 1
    while k < rows:
        valid = rowi >= k
        a_sh = jnp.where(valid, pltpu.roll(a, k, 0), 1.0)
        u_sh = jnp.where(valid, pltpu.roll(u, k, 0), 0.0)
        u = a * u_sh + u
        a = a * a_sh
        k *= 2
    h = u + a * h_ref[...]
    h_ref[...] = h[rows - 1:rows, :]
    o_ref[...] = (h * jax.nn.gelu(gate_ref[...])).astype(o_ref.dtype)


def _lru(proj, bsz, seq, cw, cb, wa, ba, wx, bx, lam, rows=256):
    nb = seq // rows
    full = lambda shape: pl.BlockSpec(shape, lambda b, c: (0,) * len(shape))
    tile_w = (LRU_WIDTH // LRU_TILE, LRU_TILE, LRU_TILE)
    return pl.pallas_call(
        functools.partial(_lru_kernel, rows=rows),
        grid=(bsz, nb),
        in_specs=[
            pl.BlockSpec((rows, LRU_WIDTH), lambda b, c: (b * nb + c, 1)),
            pl.BlockSpec((rows, LRU_WIDTH), lambda b, c: (b * nb + c, 2)),
            full((CONV_WIDTH, LRU_WIDTH)), full((1, LRU_WIDTH)),
            full(tile_w), full((1, LRU_WIDTH)), full(tile_w), full((1, LRU_WIDTH)),
            full((1, LRU_WIDTH)),
        ],
        out_specs=pl.BlockSpec((rows, LRU_WIDTH), lambda b, c: (b * nb + c, 0)),
        out_shape=jax.ShapeDtypeStruct((bsz * seq, LRU_WIDTH), BF16),
        scratch_shapes=[pltpu.VMEM((rows + SUBLANES, LRU_WIDTH), F32),
                        pltpu.VMEM((1, LRU_WIDTH), F32)],
        compiler_params=_cparams("arbitrary", "arbitrary"),
        name="lru_mixer",
    )(proj, proj, cw, cb, wa, ba, wx, bx, lam)


def _outproj_ln_kernel(*refs, n_in):
    a_refs, w_refs = refs[:n_in], refs[n_in:2 * n_in]
    res_ref, g_ref, b_ref, o_ref = refs[2 * n_in:]
    acc = ALPHA * res_ref[...]
    for a_ref, w_ref in zip(a_refs, w_refs):
        acc = acc + _dot(a_ref[...], w_ref[...])
    o_ref[...] = _layer_norm(acc, g_ref[...], b_ref[...])


def _outproj_ln(acts, weights, res, g, b, tm=512):
    t, d = res.shape
    n_in = len(acts)
    row = lambda width: pl.BlockSpec((tm, width), lambda i: (i, 0))
    return pl.pallas_call(
        functools.partial(_outproj_ln_kernel, n_in=n_in),
        grid=(t // tm,),
        in_specs=([row(a.shape[1]) for a in acts]
                  + [pl.BlockSpec(w.shape, lambda i: (0, 0)) for w in weights]
                  + [row(d), pl.BlockSpec((1, d), lambda i: (0, 0)),
                     pl.BlockSpec((1, d), lambda i: (0, 0))]),
        out_specs=row(d),
        out_shape=jax.ShapeDtypeStruct((t, d), F32),
        compiler_params=_cparams("arbitrary"),
        name="outproj_ln",
    )(*acts, *weights, res, g, b)


def _mlp_ln_kernel(x_ref, w1_ref, w2_ref, g_ref, b_ref, o_ref, acc_ref, xb_ref):
    j = pl.program_id(1)

    @pl.when(j == 0)
    def _():
        acc_ref[...] = jnp.zeros_like(acc_ref)
        xb_ref[...] = x_ref[...].astype(BF16)

    h = jnp.square(jnp.maximum(_dot(xb_ref[...], w1_ref[...]), 0.0))
    acc_ref[...] += _dot(h.astype(BF16), w2_ref[...])

    @pl.when(j == pl.num_programs(1) - 1)
    def _():
        o_ref[...] = _layer_norm(ALPHA * x_ref[...] + acc_ref[...], g_ref[...], b_ref[...])


def _mlp_ln(x2d, w1, w2, g, b, tm=512, tf=1024):
    t, d = x2d.shape
    ff = w1.shape[1]
    return pl.pallas_call(
        _mlp_ln_kernel,
        grid=(t // tm, ff // tf),
        in_specs=[pl.BlockSpec((tm, d), lambda i, j: (i, 0)),
                  pl.BlockSpec((d, tf), lambda i, j: (0, j)),
                  pl.BlockSpec((tf, d), lambda i, j: (j, 0)),
                  pl.BlockSpec((1, d), lambda i, j: (0, 0)),
                  pl.BlockSpec((1, d), lambda i, j: (0, 0))],
        out_specs=pl.BlockSpec((tm, d), lambda i, j: (i, 0)),
        out_shape=jax.ShapeDtypeStruct((t, d), F32),
        scratch_shapes=[pltpu.VMEM((tm, d), F32), pltpu.VMEM((tm, d), BF16)],
        compiler_params=_cparams("arbitrary", "arbitrary"),
        name="mlp_ln",
    )(x2d, w1, w2, g, b)


def _qkv_rope_kernel(x_ref, w_ref, pos_ref, freq_ref, s1_ref, s2_ref, o_ref):
    qk_w = DIFF_HEADS * DIFF_V_DIM
    acc = _dot(x_ref[...].astype(BF16), w_ref[...])
    ang = pos_ref[...].astype(F32) * freq_ref[...]
    cos, sin = jnp.cos(ang), jnp.sin(ang)
    reps = qk_w // LANES
    cos_w = jnp.concatenate([cos] * reps, axis=1)
    sin1_w = jnp.concatenate([sin * s1_ref[...]] * reps, axis=1)
    sin2_w = jnp.concatenate([sin * s2_ref[...]] * reps, axis=1)
    half = ROPE_DIM // 2

    def rope(t):
        return (t * cos_w + pltpu.roll(t, half, 1) * sin1_w
                + pltpu.roll(t, qk_w - half, 1) * sin2_w)

    o_ref[:, :qk_w] = (rope(acc[:, :qk_w]) * (DIFF_HEAD_DIM ** -0.5)).astype(o_ref.dtype)
    o_ref[:, qk_w:2 * qk_w] = rope(acc[:, qk_w:2 * qk_w]).astype(o_ref.dtype)
    o_ref[:, 2 * qk_w:] = acc[:, 2 * qk_w:].astype(o_ref.dtype)


def _qkv_rope(x2d, w, pos, freq, s1, s2, tm=512):
    t, d = x2d.shape
    n = w.shape[1]
    vec = pl.BlockSpec((1, LANES), lambda i: (0, 0))
    return pl.pallas_call(
        _qkv_rope_kernel,
        grid=(t // tm,),
        in_specs=[pl.BlockSpec((tm, d), lambda i: (i, 0)),
                  pl.BlockSpec((d, n), lambda i: (0, 0)),
                  pl.BlockSpec((tm, 1), lambda i: (i, 0)),
                  vec, vec, vec],
        out_specs=pl.BlockSpec((tm, n), lambda i: (i, 0)),
        out_shape=jax.ShapeDtypeStruct((t, n), BF16),
        compiler_params=_cparams("arbitrary"),
        name="qkv_rope",
    )(x2d, w, pos, freq, s1, s2)


def _attn_kernel(q_ref, k_ref, v_ref, lq1_ref, lk1_ref, lq2_ref, lk2_ref, sw_ref, o_ref,
                 m_ref, l_ref, acc_ref, *, tq, lambda_init):
    i = pl.program_id(2)
    q = q_ref[...].astype(F32)
    lane = lax.broadcasted_iota(jnp.int32, (tq, DIFF_V_DIM), 1)
    first = lane < DIFF_HEAD_DIM
    qs = (jnp.where(first, q, 0.0).astype(BF16), jnp.where(first, 0.0, q).astype(BF16))

    m_ref[...] = jnp.full_like(m_ref, -jnp.inf)
    l_ref[...] = jnp.zeros_like(l_ref)
    acc_ref[...] = jnp.zeros_like(acc_ref)

    def step(j, masked):
        start = pl.multiple_of(j * tq, tq)
        k = k_ref[pl.ds(start, tq), :]
        v = v_ref[pl.ds(start, tq), :]
        if masked:
            r = lax.broadcasted_iota(jnp.int32, (tq, tq), 0)
            c = lax.broadcasted_iota(jnp.int32, (tq, tq), 1)
            keep = c <= r
        for comp in range(2):
            s = _dot_nt(qs[comp], k)
            if masked:
                s = jnp.where(keep, s, -jnp.inf)
            m_prev = m_ref[comp]
            m_new = jnp.maximum(m_prev, jnp.max(s, axis=-1, keepdims=True))
            alpha = jnp.exp(m_prev - m_new)
            p = jnp.exp(s - m_new)
            l_ref[comp] = alpha * l_ref[comp] + jnp.sum(p, axis=-1, keepdims=True)
            acc_ref[comp] = alpha * acc_ref[comp] + _dot(p.astype(BF16), v)
            m_ref[comp] = m_new

    def body(j, carry):
        step(j, False)
        return carry

    lax.fori_loop(0, i, body, 0)
    step(i, True)

    lam = (jnp.exp(jnp.sum(lq1_ref[...] * lk1_ref[...], axis=-1, keepdims=True))
           - jnp.exp(jnp.sum(lq2_ref[...] * lk2_ref[...], axis=-1, keepdims=True)) + lambda_init)
    o = acc_ref[0] / l_ref[0] - lam * (acc_ref[1] / l_ref[1])
    o = o * lax.rsqrt(jnp.mean(o * o, axis=-1, keepdims=True) + EPS)
    o_ref[...] = (o * sw_ref[...] * (1.0 - lambda_init)).astype(o_ref.dtype)


def _attn(qkv, bsz, seq, lq1, lk1, lq2, lk2, sw, lambda_init, tq=512):
    nq = seq // tq
    hd = DIFF_V_DIM
    vec = lambda w: pl.BlockSpec((1, w), lambda b, h, i: (0, 0))
    return pl.pallas_call(
        functools.partial(_attn_kernel, tq=tq, lambda_init=lambda_init),
        grid=(bsz, DIFF_HEADS, nq),
        in_specs=[pl.BlockSpec((tq, hd), lambda b, h, i: (b * nq + i, h)),
                  pl.BlockSpec((seq, hd), lambda b, h, i: (b, DIFF_HEADS + h)),
                  pl.BlockSpec((seq, hd), lambda b, h, i: (b, 2 * DIFF_HEADS + h)),
                  vec(DIFF_HEAD_DIM), vec(DIFF_HEAD_DIM), vec(DIFF_HEAD_DIM), vec(DIFF_HEAD_DIM),
                  vec(hd)],
        out_specs=pl.BlockSpec((tq, hd), lambda b, h, i: (b * nq + i, h)),
        out_shape=jax.ShapeDtypeStruct((bsz * seq, DIFF_HEADS * hd), BF16),
        scratch_shapes=[pltpu.VMEM((2, tq, 1), F32), pltpu.VMEM((2, tq, 1), F32),
                        pltpu.VMEM((2, tq, hd), F32)],
        compiler_params=_cparams("arbitrary", "arbitrary", "arbitrary"),
        name="diff_attn",
    )(qkv, qkv, qkv, lq1, lk1, lq2, lk2, sw)


def _block_diag_tiles(w):
    per = LRU_TILE // LRU_BLOCK
    w4 = w.reshape(LRU_HEADS // per, per, LRU_BLOCK, LRU_BLOCK)
    bd = jnp.einsum('tgij,gh->tgihj', w4, jnp.eye(per, dtype=w.dtype))
    return bd.reshape(LRU_HEADS // per, LRU_TILE, LRU_TILE)


def _pad_lanes(v):
    return jnp.pad(v, (0, LANES - v.shape[0]))[None, :]


def kernel(x, positions, ssm_w_in, ssm_conv_w, ssm_conv_b, ssm_dt_bias, ssm_a_log, ssm_d, ssm_norm_w, lru_conv_w, lru_conv_b, lru_w_a, lru_b_a, lru_w_x, lru_b_x, lru_lambda, mix_w_out, attn_w_qkv, attn_lq1, attn_lk1, attn_lq2, attn_lk2, attn_subln_w, attn_w_out, ln1_g, ln1_b, ff_w1, ff_w2, ln2_g, ln2_b):
    bsz, seq, d = x.shape
    h = x.reshape(bsz * seq, d)
    row = lambda v: v[None, :]

    head_of_lane = jnp.arange(SSD_INNER) // SSD_HEAD_DIM
    expand = (jnp.arange(LANES)[:, None] == head_of_lane[None, :]).astype(BF16)

    inv_freq = ROPE_THETA ** (-jnp.arange(0, ROPE_DIM, 2, dtype=F32) / ROPE_DIM)
    lane = jnp.arange(LANES)
    in_head = lane % DIFF_HEAD_DIM
    freq = jnp.where(in_head < ROPE_DIM, inv_freq[lane % (ROPE_DIM // 2)], 0.0)[None, :]
    sin_up = ((in_head >= ROPE_DIM // 2) & (in_head < ROPE_DIM)).astype(F32)[None, :]
    sin_dn = -(in_head < ROPE_DIM // 2).astype(F32)[None, :]
    pos = positions.reshape(bsz * seq, 1)

    for layer in range(DEPTH):
        i = layer // 2
        if layer % 2 == 0:
            w_in = ssm_w_in[i]
            s0, s1, s2, s3 = (SSD_INNER, SSD_INNER + SSD_CONV_DIM,
                              SSD_INNER + SSD_CONV_DIM + SSD_HEADS,
                              SSD_INNER + SSD_CONV_DIM + SSD_HEADS + LRU_WIDTH)
            w_perm = jnp.concatenate(
                [w_in[:, :s0], w_in[:, s2:s3], w_in[:, s3:], w_in[:, s0:s1], w_in[:, s1:s2],
                 jnp.zeros((d, LANES - SSD_HEADS), w_in.dtype)], axis=1).astype(BF16)
            proj = _proj(h, w_perm)
            y_a = _ssd(proj, bsz, seq, ssm_conv_w[i], row(ssm_conv_b[i]),
                       _pad_lanes(ssm_dt_bias[i]), _pad_lanes(ssm_a_log[i]),
                       row(jnp.repeat(ssm_d[i], SSD_HEAD_DIM)), row(ssm_norm_w[i]), expand)
            y_b = _lru(proj, bsz, seq, lru_conv_w[i], row(lru_conv_b[i]),
                       _block_diag_tiles(lru_w_a[i]).astype(BF16), row(lru_b_a[i].reshape(-1)),
                       _block_diag_tiles(lru_w_x[i]).astype(BF16), row(lru_b_x[i].reshape(-1)),
                       row(lru_lambda[i]))
            w_out = mix_w_out[i].astype(BF16)
            h = _outproj_ln([y_a, y_b], [w_out[:SSD_INNER], w_out[SSD_INNER:]], h,
                            row(ln1_g[layer]), row(ln1_b[layer]))
        else:
            lambda_init = 0.8 - 0.6 * math.exp(-0.3 * layer)
            qkv = _qkv_rope(h, attn_w_qkv[i].astype(BF16), pos, freq, sin_up, sin_dn)
            o = _attn(qkv, bsz, seq, row(attn_lq1[i]), row(attn_lk1[i]), row(attn_lq2[i]),
                      row(attn_lk2[i]), row(attn_subln_w[i]), lambda_init)
            h = _outproj_ln([o], [attn_w_out[i].astype(BF16)], h,
                            row(ln1_g[layer]), row(ln1_b[layer]))
        h = _mlp_ln(h, ff_w1[layer].astype(BF16), ff_w2[layer].astype(BF16),
                    row(ln2_g[layer]), row(ln2_b[layer]))
    return h.reshape(bsz, seq, d)
```

```python
import functools
import math

import jax
import jax.numpy as jnp
from jax import lax
from jax.experimental import pallas as pl
from jax.experimental.pallas import tpu as pltpu

F32 = jnp.float32
BF16 = jnp.bfloat16

SUBLANES = 8
LANES = 128
BF16_ROWS = 2 * SUBLANES
LOG2_E = math.log2(math.e)

D_MODEL = 1024
DEPTH = 2
SSD_HEADS = 16
SSD_HEAD_DIM = 64
SSD_INNER = SSD_HEADS * SSD_HEAD_DIM
SSD_GROUPS = 2
SSD_STATE = 128
SSD_CHUNK = 128
SSD_CONV_DIM = SSD_INNER + 2 * SSD_GROUPS * SSD_STATE
CONV_WIDTH = 4
LRU_HEADS = 16
LRU_WIDTH = D_MODEL
LRU_BLOCK = LRU_WIDTH // LRU_HEADS
LRU_C = 8.0
LRU_TILE = 256
DIFF_HEADS = 8
DIFF_HEAD_DIM = 64
DIFF_V_DIM = 2 * DIFF_HEAD_DIM
ROPE_DIM = DIFF_HEAD_DIM // 4
ROPE_THETA = 500000.0
ATTN_BLOCK = 512
D_FF = 4 * D_MODEL
ALPHA = (2 * DEPTH) ** 0.25
EPS = 1e-5

PROJ_COLS = 3 * D_MODEL + SSD_CONV_DIM + LANES

VMEM_LIMIT = 56 * 1024 * 1024


def _cparams(*sem):
    return pltpu.CompilerParams(dimension_semantics=sem, vmem_limit_bytes=VMEM_LIMIT)


def _layer_norm(v, g, b):
    mu = jnp.mean(v, axis=-1, keepdims=True)
    d = v - mu
    var = jnp.mean(d * d, axis=-1, keepdims=True)
    return d * lax.rsqrt(var + EPS) * g + b


def _dot(a, b):
    return jnp.dot(a, b, preferred_element_type=F32)


def _dot_nt(a, b):
    return lax.dot_general(a, b, (((1,), (1,)), ((), ())), preferred_element_type=F32)


def _proj_kernel(x_ref, w_ref, o_ref):
    o_ref[...] = _dot(x_ref[...].astype(BF16), w_ref[...])


def _proj(x2d, w, tm=256):
    t, d = x2d.shape
    n = w.shape[1]
    return pl.pallas_call(
        _proj_kernel,
        grid=(t // tm,),
        in_specs=[pl.BlockSpec((tm, d), lambda i: (i, 0)),
                  pl.BlockSpec((d, n), lambda i: (0, 0))],
        out_specs=pl.BlockSpec((tm, n), lambda i: (i, 0)),
        out_shape=jax.ShapeDtypeStruct((t, n), F32),
        compiler_params=_cparams("arbitrary"),
        name="in_proj",
    )(x2d, w)


def _causal_conv(x_ref, xpad_ref, cw_ref, cb_ref, rows):
    xpad_ref[SUBLANES:SUBLANES + rows, :] = x_ref[...]
    out = cb_ref[...]
    for k in range(CONV_WIDTH):
        start = SUBLANES - (CONV_WIDTH - 1) + k
        out = out + cw_ref[k:k + 1, :] * xpad_ref[start:start + rows, :]
    xpad_ref[0:SUBLANES, :] = xpad_ref[rows:rows + SUBLANES, :]
    return out


def _expand_heads(v, e):
    hi = v.astype(BF16)
    lo = (v - hi.astype(F32)).astype(BF16)
    return _dot(hi, e) + _dot(lo, e)


def _ssd_kernel(z_ref, xbc_ref, dt_ref, cw_ref, cb_ref, dtb_ref, alog_ref, dsk_ref, nw_ref, e_ref,
                o_ref, xpad_ref, st_ref):
    l = SSD_CHUNK
    gw = SSD_INNER // SSD_GROUPS
    c = pl.program_id(1)

    @pl.when(c == 0)
    def _():
        xpad_ref[0:SUBLANES, :] = jnp.zeros((SUBLANES, SSD_CONV_DIM), F32)
        st_ref[...] = jnp.zeros_like(st_ref)

    xbc = jax.nn.silu(_causal_conv(xbc_ref, xpad_ref, cw_ref, cb_ref, l))
    xs = xbc[:, :SSD_INNER]
    bm = xbc[:, SSD_INNER:SSD_INNER + SSD_GROUPS * SSD_STATE]
    cm = xbc[:, SSD_INNER + SSD_GROUPS * SSD_STATE:]

    dt = jax.nn.softplus(dt_ref[...] + dtb_ref[...])
    adt = dt * (-jnp.exp(alog_ref[...]))
    row = lax.broadcasted_iota(jnp.int32, (l, l), 0)
    col = lax.broadcasted_iota(jnp.int32, (l, l), 1)
    causal = row >= col
    a_cum = jnp.dot(causal.astype(F32), adt, precision=lax.Precision.HIGHEST,
                    preferred_element_type=F32)
    a_cum_t = a_cum.T
    a_last = a_cum[l - 1:l, :]
    ea = jnp.exp(a_cum)
    stacked = jnp.concatenate([dt, ea, dt * jnp.exp(a_last - a_cum)], axis=0)
    ex = _expand_heads(stacked, e_ref[...])
    dt_x, ea_x, ds_x = ex[:l], ex[l:2 * l], ex[2 * l:]
    xdt = xs * dt_x
    x2 = (xs * ds_x).astype(BF16)
    cd_x = ea_x[l - 1:l, :]

    lane = lax.broadcasted_iota(jnp.int32, (l, LANES), 1)
    lo_half = lane < SSD_HEAD_DIM
    y_parts = []
    for g in range(SSD_GROUPS):
        bg = bm[:, g * SSD_STATE:(g + 1) * SSD_STATE]
        cg = cm[:, g * SSD_STATE:(g + 1) * SSD_STATE].astype(BF16)
        cb = _dot_nt(cg, bg.astype(BF16))
        heads_per_group = SSD_HEADS // SSD_GROUPS
        diag = []
        for pair in range(heads_per_group // 2):
            blk = g * (heads_per_group // 2) + pair
            xp = xdt[:, blk * LANES:(blk + 1) * LANES]
            acc = None
            for half in range(2):
                h = 2 * blk + half
                seg = a_cum[:, h:h + 1] - a_cum_t[h:h + 1, :]
                dec = jnp.exp(jnp.where(causal, seg, -jnp.inf))
                m = (cb * dec).astype(BF16)
                keep = lo_half if half == 0 else jnp.logical_not(lo_half)
                part = _dot(m, jnp.where(keep, xp, 0.0).astype(BF16))
                acc = part if acc is None else acc + part
            diag.append(acc)
        y_diag = jnp.concatenate(diag, axis=1)
        prev = st_ref[g]
        y_off = _dot(cg, prev.astype(BF16)) * ea_x[:, g * gw:(g + 1) * gw]
        st_new = _dot(bg.T.astype(BF16), x2[:, g * gw:(g + 1) * gw])
        st_ref[g] = prev * cd_x[:, g * gw:(g + 1) * gw] + st_new
        y_parts.append(y_diag + y_off)

    y = jnp.concatenate(y_parts, axis=1) + dsk_ref[...] * xs
    y = y * jax.nn.silu(z_ref[...])
    normed = []
    for g in range(SSD_GROUPS):
        yg = y[:, g * gw:(g + 1) * gw]
        normed.append(yg * lax.rsqrt(jnp.mean(yg * yg, axis=-1, keepdims=True) + EPS))
    o_ref[...] = (jnp.concatenate(normed, axis=1) * nw_ref[...]).astype(o_ref.dtype)


def _ssd(proj, bsz, seq, cw, cb, dtb, alog, dsk, nw, e):
    l = SSD_CHUNK
    nc = seq // l
    full = lambda shape: pl.BlockSpec(shape, lambda b, c: (0,) * len(shape))
    xbc_blk = (3 * D_MODEL) // SSD_CONV_DIM
    dt_blk = (3 * D_MODEL + SSD_CONV_DIM) // LANES
    return pl.pallas_call(
        _ssd_kernel,
        grid=(bsz, nc),
        in_specs=[
            pl.BlockSpec((l, SSD_INNER), lambda b, c: (b * nc + c, 0)),
            pl.BlockSpec((l, SSD_CONV_DIM), lambda b, c: (b * nc + c, xbc_blk)),
            pl.BlockSpec((l, LANES), lambda b, c: (b * nc + c, dt_blk)),
            full((CONV_WIDTH, SSD_CONV_DIM)), full((1, SSD_CONV_DIM)),
            full((1, LANES)), full((1, LANES)),
            full((1, SSD_INNER)), full((1, SSD_INNER)),
            full((LANES, SSD_INNER)),
        ],
        out_specs=pl.BlockSpec((l, SSD_INNER), lambda b, c: (b * nc + c, 0)),
        out_shape=jax.ShapeDtypeStruct((bsz * seq, SSD_INNER), BF16),
        scratch_shapes=[pltpu.VMEM((l + SUBLANES, SSD_CONV_DIM), F32),
                        pltpu.VMEM((SSD_GROUPS, SSD_STATE, SSD_INNER // SSD_GROUPS), F32)],
        compiler_params=_cparams("arbitrary", "arbitrary"),
        name="ssd_mixer",
    )(proj, proj, proj, cw, cb, dtb, alog, dsk, nw, e)


def _lru_kernel(gate_ref, xl_ref, cw_ref, cb_ref, wa_ref, ba_ref, wx_ref, bx_ref, lam_ref,
                o_ref, xpad_ref, h_ref, *, rows):
    c = pl.program_id(1)

    @pl.when(c == 0)
    def _():
        xpad_ref[0:SUBLANES, :] = jnp.zeros((SUBLANES, LRU_WIDTH), F32)
        h_ref[...] = jnp.zeros_like(h_ref)

    xl = _causal_conv(xl_ref, xpad_ref, cw_ref, cb_ref, rows)
    xlb = xl.astype(BF16)
    n_tiles = LRU_WIDTH // LRU_TILE

    def gate(w_ref, b_ref):
        parts = [_dot(xlb[:, t * LRU_TILE:(t + 1) * LRU_TILE], w_ref[t]) for t in range(n_tiles)]
        return jax.nn.sigmoid(jnp.concatenate(parts, axis=1) + b_ref[...])

    r = gate(wa_ref, ba_ref)
    i = gate(wx_ref, bx_ref)
    log_a = -LRU_C * r * jax.nn.softplus(-lam_ref[...])
    a = jnp.exp(log_a)
    mult = jnp.sqrt(jnp.maximum(1.0 - jnp.exp(2.0 * log_a), 0.0))
    u = mult * (i * xl)

    groups = rows // SUBLANES
    a = a.reshape(groups, SUBLANES, LRU_WIDTH)
    u = u.reshape(groups, SUBLANES, LRU_WIDTH)
    sub = lax.broadcasted_iota(jnp.int32, (groups, SUBLANES, LRU_WIDTH), 1)
    k = 1
    while k < SUBLANES:
        valid = sub >= k
        a_sh = jnp.where(valid, pltpu.roll(a, k, 1), 1.0)
        u_sh = jnp.where(valid, pltpu.roll(u, k, 1), 0.0)
        u = a * u_sh + u
        a = a * a_sh
        k *= 2
    carry = h_ref[...]
    hs = []
    for gi in range(groups):
        hg = u[gi] + a[gi] * carry
        hs.append(hg)
        carry = jnp.broadcast_to(hg[SUBLANES - 1:SUBLANES, :], (SUBLANES, LRU_WIDTH))
    h_ref[...] = carry
    h = jnp.concatenate(hs, axis=0)
    o_ref[...] = (h * jax.nn.gelu(gate_ref[...])).astype(o_ref.dtype)


def _lru(proj, bsz, seq, cw, cb, wa, ba, wx, bx, lam, rows=256):
    nb = seq // rows
    full = lambda shape: pl.BlockSpec(shape, lambda b, c: (0,) * len(shape))
    tile_w = (LRU_WIDTH // LRU_TILE, LRU_TILE, LRU_TILE)
    return pl.pallas_call(
        functools.partial(_lru_kernel, rows=rows),
        grid=(bsz, nb),
        in_specs=[
            pl.BlockSpec((rows, LRU_WIDTH), lambda b, c: (b * nb + c, 1)),
            pl.BlockSpec((rows, LRU_WIDTH), lambda b, c: (b * nb + c, 2)),
            full((CONV_WIDTH, LRU_WIDTH)), full((1, LRU_WIDTH)),
            full(tile_w), full((1, LRU_WIDTH)), full(tile_w), full((1, LRU_WIDTH)),
            full((1, LRU_WIDTH)),
        ],
        out_specs=pl.BlockSpec((rows, LRU_WIDTH), lambda b, c: (b * nb + c, 0)),
        out_shape=jax.ShapeDtypeStruct((bsz * seq, LRU_WIDTH), BF16),
        scratch_shapes=[pltpu.VMEM((rows + SUBLANES, LRU_WIDTH), F32),
                        pltpu.VMEM((SUBLANES, LRU_WIDTH), F32)],
        compiler_params=_cparams("arbitrary", "arbitrary"),
        name="lru_mixer",
    )(proj, proj, cw, cb, wa, ba, wx, bx, lam)


def _outproj_ln_kernel(*refs, n_in):
    a_refs, w_refs = refs[:n_in], refs[n_in:2 * n_in]
    res_ref, g_ref, b_ref, o_ref = refs[2 * n_in:]
    acc = ALPHA * res_ref[...]
    for a_ref, w_ref in zip(a_refs, w_refs):
        acc = acc + _dot(a_ref[...], w_ref[...])
    o_ref[...] = _layer_norm(acc, g_ref[...], b_ref[...])


def _outproj_ln(acts, weights, res, g, b, tm=512):
    t, d = res.shape
    n_in = len(acts)
    row = lambda width: pl.BlockSpec((tm, width), lambda i: (i, 0))
    return pl.pallas_call(
        functools.partial(_outproj_ln_kernel, n_in=n_in),
        grid=(t // tm,),
        in_specs=([row(a.shape[1]) for a in acts]
                  + [pl.BlockSpec(w.shape, lambda i: (0, 0)) for w in weights]
                  + [row(d), pl.BlockSpec((1, d), lambda i: (0, 0)),
                     pl.BlockSpec((1, d), lambda i: (0, 0))]),
        out_specs=row(d),
        out_shape=jax.ShapeDtypeStruct((t, d), F32),
        compiler_params=_cparams("arbitrary"),
        name="outproj_ln",
    )(*acts, *weights, res, g, b)


def _mlp_ln_kernel(x_ref, w1_ref, w2_ref, g_ref, b_ref, o_ref, acc_ref, xb_ref):
    j = pl.program_id(1)

    @pl.when(j == 0)
    def _():
        acc_ref[...] = jnp.zeros_like(acc_ref)
        xb_ref[...] = x_ref[...].astype(BF16)

    h = jnp.square(jnp.maximum(_dot(xb_ref[...], w1_ref[...]), 0.0))
    acc_ref[...] += _dot(h.astype(BF16), w2_ref[...])

    @pl.when(j == pl.num_programs(1) - 1)
    def _():
        o_ref[...] = _layer_norm(ALPHA * x_ref[...] + acc_ref[...], g_ref[...], b_ref[...])


def _mlp_ln(x2d, w1, w2, g, b, tm=512, tf=1024):
    t, d = x2d.shape
    ff = w1.shape[1]
    return pl.pallas_call(
        _mlp_ln_kernel,
        grid=(t // tm, ff // tf),
        in_specs=[pl.BlockSpec((tm, d), lambda i, j: (i, 0)),
                  pl.BlockSpec((d, tf), lambda i, j: (0, j)),
                  pl.BlockSpec((tf, d), lambda i, j: (j, 0)),
                  pl.BlockSpec((1, d), lambda i, j: (0, 0)),
                  pl.BlockSpec((1, d), lambda i, j: (0, 0))],
        out_specs=pl.BlockSpec((tm, d), lambda i, j: (i, 0)),
        out_shape=jax.ShapeDtypeStruct((t, d), F32),
        scratch_shapes=[pltpu.VMEM((tm, d), F32), pltpu.VMEM((tm, d), BF16)],
        compiler_params=_cparams("arbitrary", "arbitrary"),
        name="mlp_ln",
    )(x2d, w1, w2, g, b)


def _qkv_rope_kernel(x_ref, w_ref, pos_ref, freq_ref, s1_ref, s2_ref, qt_ref, k_ref, vt_ref):
    qk_w = DIFF_HEADS * DIFF_V_DIM
    acc = _dot(x_ref[...].astype(BF16), w_ref[...])
    ang = pos_ref[...].astype(F32) * freq_ref[...]
    cos, sin = jnp.cos(ang), jnp.sin(ang)
    reps = qk_w // LANES
    cos_w = jnp.concatenate([cos] * reps, axis=1)
    sin1_w = jnp.concatenate([sin * s1_ref[...]] * reps, axis=1)
    sin2_w = jnp.concatenate([sin * s2_ref[...]] * reps, axis=1)
    half = ROPE_DIM // 2

    def rope(t):
        return (t * cos_w + pltpu.roll(t, half, 1) * sin1_w
                + pltpu.roll(t, qk_w - half, 1) * sin2_w)

    rows = acc.shape[0]
    q = rope(acc[:, :qk_w]) * (DIFF_HEAD_DIM ** -0.5 * LOG2_E)
    qt_ref[...] = q.T.astype(qt_ref.dtype).reshape(DIFF_HEADS, DIFF_V_DIM, rows)
    k_ref[...] = rope(acc[:, qk_w:2 * qk_w]).astype(k_ref.dtype)
    vt_ref[...] = acc[:, 2 * qk_w:].T.astype(vt_ref.dtype).reshape(DIFF_HEADS, DIFF_V_DIM, rows)


def _qkv_rope(x2d, w, pos, freq, s1, s2, bsz, seq, blk):
    t, d = x2d.shape
    n = w.shape[1]
    nb = seq // blk
    vec = pl.BlockSpec((1, LANES), lambda i: (0, 0))
    tr_spec = pl.BlockSpec((None, DIFF_HEADS, None, DIFF_V_DIM, blk),
                           lambda i: (i // nb, 0, i % nb, 0, 0))
    tr_shape = jax.ShapeDtypeStruct((bsz, DIFF_HEADS, nb, DIFF_V_DIM, blk), BF16)
    return pl.pallas_call(
        _qkv_rope_kernel,
        grid=(t // blk,),
        in_specs=[pl.BlockSpec((blk, d), lambda i: (i, 0)),
                  pl.BlockSpec((d, n), lambda i: (0, 0)),
                  pl.BlockSpec((blk, 1), lambda i: (i, 0)),
                  vec, vec, vec],
        out_specs=[tr_spec, pl.BlockSpec((blk, n // 3), lambda i: (i, 0)), tr_spec],
        out_shape=[tr_shape, jax.ShapeDtypeStruct((t, n // 3), BF16), tr_shape],
        compiler_params=_cparams("arbitrary"),
        name="qkv_rope",
    )(x2d, w, pos, freq, s1, s2)


def _attn_kernel(qt_ref, k_ref, vt_ref, lq1_ref, lk1_ref, lq2_ref, lk2_ref, sw_ref, o_ref,
                 m_ref, acc_ref, sta_ref, stb_ref, *, blk, lambda_init):
    i = pl.program_id(2)
    qt = qt_ref[...].astype(F32)
    first = lax.broadcasted_iota(jnp.int32, (DIFF_V_DIM, blk), 0) < DIFF_HEAD_DIM
    qts = (jnp.where(first, qt, 0.0).astype(BF16), jnp.where(first, 0.0, qt).astype(BF16))

    m_ref[...] = jnp.full_like(m_ref, -jnp.inf)
    acc_ref[...] = jnp.zeros_like(acc_ref)
    ones_rows = jnp.ones((BF16_ROWS, blk), BF16)

    def scores(j, st_ref):
        start = pl.multiple_of(j * blk, blk)
        k = k_ref[pl.ds(start, blk), :]
        for comp in range(2):
            st_ref[comp] = _dot(k, qts[comp])

    def accumulate(j, st_ref, masked):
        vt = jnp.concatenate([vt_ref[j], ones_rows], axis=0)
        if masked:
            key = lax.broadcasted_iota(jnp.int32, (blk, blk), 0)
            qry = lax.broadcasted_iota(jnp.int32, (blk, blk), 1)
            keep = key <= qry
        for comp in range(2):
            st = st_ref[comp]
            if masked:
                st = jnp.where(keep, st, -jnp.inf)
            m_prev = m_ref[comp]
            m_new = jnp.maximum(m_prev, jnp.max(st, axis=0, keepdims=True))
            alpha = jnp.exp2(m_prev - m_new)
            p = jnp.exp2(st - m_new)
            acc_ref[comp] = alpha * acc_ref[comp] + _dot(vt, p.astype(BF16))
            m_ref[comp] = m_new

    def pair(t, carry):
        j = 2 * t
        scores(j + 1, stb_ref)
        accumulate(j, sta_ref, False)
        scores(j + 2, sta_ref)
        accumulate(j + 1, stb_ref, False)
        return carry

    scores(0, sta_ref)
    lax.fori_loop(0, i // 2, pair, 0)

    @pl.when(i % 2 == 1)
    def _():
        scores(i, stb_ref)
        accumulate(i - 1, sta_ref, False)
        accumulate(i, stb_ref, True)

    @pl.when(i % 2 == 0)
    def _():
        accumulate(i, sta_ref, True)

    lam = (jnp.exp(jnp.sum(lq1_ref[...] * lk1_ref[...], axis=-1, keepdims=True))
           - jnp.exp(jnp.sum(lq2_ref[...] * lk2_ref[...], axis=-1, keepdims=True)) + lambda_init)
    hd = DIFF_V_DIM
    outs = [acc_ref[comp, :hd, :] / acc_ref[comp, hd:hd + 1, :] for comp in range(2)]
    ot = outs[0] - lam * outs[1]
    ot = ot * lax.rsqrt(jnp.mean(ot * ot, axis=0, keepdims=True) + EPS)
    ot = ot * sw_ref[...] * (1.0 - lambda_init)
    o_ref[...] = ot.T.astype(o_ref.dtype)


def _attn(qt, k, vt, lq1, lk1, lq2, lk2, sw_col, lambda_init):
    bsz, heads, nb, hd, blk = qt.shape
    seq = nb * blk
    vec = lambda w: pl.BlockSpec((1, w), lambda b, h, i: (0, 0))
    return pl.pallas_call(
        functools.partial(_attn_kernel, blk=blk, lambda_init=lambda_init),
        grid=(bsz, heads, nb),
        in_specs=[pl.BlockSpec((None, None, None, hd, blk), lambda b, h, i: (b, h, i, 0, 0)),
                  pl.BlockSpec((seq, hd), lambda b, h, i: (b, h)),
                  pl.BlockSpec((None, None, nb, hd, blk), lambda b, h, i: (b, h, 0, 0, 0)),
                  vec(DIFF_HEAD_DIM), vec(DIFF_HEAD_DIM), vec(DIFF_HEAD_DIM), vec(DIFF_HEAD_DIM),
                  pl.BlockSpec((hd, 1), lambda b, h, i: (0, 0))],
        out_specs=pl.BlockSpec((blk, hd), lambda b, h, i: (b * nb + i, h)),
        out_shape=jax.ShapeDtypeStruct((bsz * seq, heads * hd), BF16),
        scratch_shapes=[pltpu.VMEM((2, 1, blk), F32),
                        pltpu.VMEM((2, hd + BF16_ROWS, blk), F32),
                        pltpu.VMEM((2, blk, blk), F32), pltpu.VMEM((2, blk, blk), F32)],
        compiler_params=_cparams("arbitrary", "arbitrary", "arbitrary"),
        name="diff_attn",
    )(qt, k, vt, lq1, lk1, lq2, lk2, sw_col)


def _block_diag_tiles(w):
    per = LRU_TILE // LRU_BLOCK
    w4 = w.reshape(LRU_HEADS // per, per, LRU_BLOCK, LRU_BLOCK)
    bd = jnp.einsum('tgij,gh->tgihj', w4, jnp.eye(per, dtype=w.dtype))
    return bd.reshape(LRU_HEADS // per, LRU_TILE, LRU_TILE)


def _pad_lanes(v):
    return jnp.pad(v, (0, LANES - v.shape[0]))[None, :]


def kernel(x, positions, ssm_w_in, ssm_conv_w, ssm_conv_b, ssm_dt_bias, ssm_a_log, ssm_d, ssm_norm_w, lru_conv_w, lru_conv_b, lru_w_a, lru_b_a, lru_w_x, lru_b_x, lru_lambda, mix_w_out, attn_w_qkv, attn_lq1, attn_lk1, attn_lq2, attn_lk2, attn_subln_w, attn_w_out, ln1_g, ln1_b, ff_w1, ff_w2, ln2_g, ln2_b):
    bsz, seq, d = x.shape
    h = x.reshape(bsz * seq, d)
    row = lambda v: v[None, :]

    head_of_lane = jnp.arange(SSD_INNER) // SSD_HEAD_DIM
    expand = (jnp.arange(LANES)[:, None] == head_of_lane[None, :]).astype(BF16)

    inv_freq = ROPE_THETA ** (-jnp.arange(0, ROPE_DIM, 2, dtype=F32) / ROPE_DIM)
    lane = jnp.arange(LANES)
    in_head = lane % DIFF_HEAD_DIM
    freq = jnp.where(in_head < ROPE_DIM, inv_freq[lane % (ROPE_DIM // 2)], 0.0)[None, :]
    sin_up = ((in_head >= ROPE_DIM // 2) & (in_head < ROPE_DIM)).astype(F32)[None, :]
    sin_dn = -(in_head < ROPE_DIM // 2).astype(F32)[None, :]
    pos = positions.reshape(bsz * seq, 1)

    for layer in range(DEPTH):
        i = layer // 2
        if layer % 2 == 0:
            w_in = ssm_w_in[i]
            s0, s1, s2, s3 = (SSD_INNER, SSD_INNER + SSD_CONV_DIM,
                              SSD_INNER + SSD_CONV_DIM + SSD_HEADS,
                              SSD_INNER + SSD_CONV_DIM + SSD_HEADS + LRU_WIDTH)
            w_perm = jnp.concatenate(
                [w_in[:, :s0], w_in[:, s2:s3], w_in[:, s3:], w_in[:, s0:s1], w_in[:, s1:s2],
                 jnp.zeros((d, LANES - SSD_HEADS), w_in.dtype)], axis=1).astype(BF16)
            proj = _proj(h, w_perm)
            y_a = _ssd(proj, bsz, seq, ssm_conv_w[i], row(ssm_conv_b[i]),
                       _pad_lanes(ssm_dt_bias[i]), _pad_lanes(ssm_a_log[i]),
                       row(jnp.repeat(ssm_d[i], SSD_HEAD_DIM)), row(ssm_norm_w[i]), expand)
            y_b = _lru(proj, bsz, seq, lru_conv_w[i], row(lru_conv_b[i]),
                       _block_diag_tiles(lru_w_a[i]).astype(BF16), row(lru_b_a[i].reshape(-1)),
                       _block_diag_tiles(lru_w_x[i]).astype(BF16), row(lru_b_x[i].reshape(-1)),
                       row(lru_lambda[i]))
            w_out = mix_w_out[i].astype(BF16)
            h = _outproj_ln([y_a, y_b], [w_out[:SSD_INNER], w_out[SSD_INNER:]], h,
                            row(ln1_g[layer]), row(ln1_b[layer]))
        else:
            lambda_init = 0.8 - 0.6 * math.exp(-0.3 * layer)
            qt, k, vt = _qkv_rope(h, attn_w_qkv[i].astype(BF16), pos, freq, sin_up, sin_dn,
                                  bsz, seq, ATTN_BLOCK)
            o = _attn(qt, k, vt, row(attn_lq1[i]), row(attn_lk1[i]), row(attn_lq2[i]),
                      row(attn_lk2[i]), attn_subln_w[i][:, None], lambda_init)
            h = _outproj_ln([o], [attn_w_out[i].astype(BF16)], h,
                            row(ln1_g[layer]), row(ln1_b[layer]))
        h = _mlp_ln(h, ff_w1[layer].astype(BF16), ff_w2[layer].astype(BF16),
                    row(ln2_g[layer]), row(ln2_b[layer]))
    return h.reshape(bsz, seq, d)
```

```python
import functools
import math

import jax
import jax.numpy as jnp
from jax import lax
from jax.experimental import pallas as pl
from jax.experimental.pallas import tpu as pltpu

F32 = jnp.float32
BF16 = jnp.bfloat16

SUBLANES = 8
LANES = 128
BF16_ROWS = 2 * SUBLANES
LOG2_E = math.log2(math.e)

D_MODEL = 1024
DEPTH = 2
SSD_HEADS = 16
SSD_HEAD_DIM = 64
SSD_INNER = SSD_HEADS * SSD_HEAD_DIM
SSD_GROUPS = 2
SSD_STATE = 128
SSD_CHUNK = 128
SSD_CONV_DIM = SSD_INNER + 2 * SSD_GROUPS * SSD_STATE
CONV_WIDTH = 4
LRU_HEADS = 16
LRU_WIDTH = D_MODEL
LRU_BLOCK = LRU_WIDTH // LRU_HEADS
LRU_C = 8.0
LRU_TILE = 256
DIFF_HEADS = 8
DIFF_HEAD_DIM = 64
DIFF_V_DIM = 2 * DIFF_HEAD_DIM
ROPE_DIM = DIFF_HEAD_DIM // 4
ROPE_THETA = 500000.0
ATTN_BLOCK = 512
ATTN_Q_RATIO = 4
D_FF = 4 * D_MODEL
ALPHA = (2 * DEPTH) ** 0.25
EPS = 1e-5

PROJ_COLS = 3 * D_MODEL + SSD_CONV_DIM + LANES
MIX_ROWS = 256
PROJ_PIECE = 256
PROJ_TICKS_PER_PIECE = 2

VMEM_LIMIT = 56 * 1024 * 1024


def _cparams(*sem):
    return pltpu.CompilerParams(dimension_semantics=sem, vmem_limit_bytes=VMEM_LIMIT)


def _layer_norm(v, g, b):
    mu = jnp.mean(v, axis=-1, keepdims=True)
    d = v - mu
    var = jnp.mean(d * d, axis=-1, keepdims=True)
    return d * lax.rsqrt(var + EPS) * g + b


def _dot(a, b):
    return jnp.dot(a, b, preferred_element_type=F32)


def _dot_nt(a, b):
    return lax.dot_general(a, b, (((1,), (1,)), ((), ())), preferred_element_type=F32)


def _causal_conv(x_ref, xpad_ref, cw_ref, cb_ref, rows):
    xpad_ref[SUBLANES:SUBLANES + rows, :] = x_ref[...]
    out = cb_ref[...]
    for k in range(CONV_WIDTH):
        start = SUBLANES - (CONV_WIDTH - 1) + k
        out = out + cw_ref[k:k + 1, :] * xpad_ref[start:start + rows, :]
    xpad_ref[0:SUBLANES, :] = xpad_ref[rows:rows + SUBLANES, :]
    return out


def _expand_heads(v, e):
    hi = v.astype(BF16)
    lo = (v - hi.astype(F32)).astype(BF16)
    return _dot(hi, e) + _dot(lo, e)


def _ssd_chunk(z_ref, xbc_ref, dt_ref, cw_ref, cb_ref, dtb_ref, alog_ref, dsk_ref, nw_ref, e_ref,
               xpad_ref, st_ref, tick):
    l = SSD_CHUNK
    gw = SSD_INNER // SSD_GROUPS
    tick()
    xbc = jax.nn.silu(_causal_conv(xbc_ref, xpad_ref, cw_ref, cb_ref, l))
    tick()
    xs = xbc[:, :SSD_INNER]
    bm = xbc[:, SSD_INNER:SSD_INNER + SSD_GROUPS * SSD_STATE]
    cm = xbc[:, SSD_INNER + SSD_GROUPS * SSD_STATE:]

    dt = jax.nn.softplus(dt_ref[...] + dtb_ref[...])
    adt = dt * (-jnp.exp(alog_ref[...]))
    row = lax.broadcasted_iota(jnp.int32, (l, l), 0)
    col = lax.broadcasted_iota(jnp.int32, (l, l), 1)
    causal = row >= col
    a_cum = jnp.dot(causal.astype(F32), adt, precision=lax.Precision.HIGHEST,
                    preferred_element_type=F32)
    a_cum_t = a_cum.T
    a_last = a_cum[l - 1:l, :]
    ea = jnp.exp(a_cum)
    stacked = jnp.concatenate([dt, ea, dt * jnp.exp(a_last - a_cum)], axis=0)
    ex = _expand_heads(stacked, e_ref[...])
    dt_x, ea_x, ds_x = ex[:l], ex[l:2 * l], ex[2 * l:]
    xdt = xs * dt_x
    x2 = (xs * ds_x).astype(BF16)
    cd_x = ea_x[l - 1:l, :]

    lane = lax.broadcasted_iota(jnp.int32, (l, LANES), 1)
    lo_half = lane < SSD_HEAD_DIM
    y_parts = []
    for g in range(SSD_GROUPS):
        bg = bm[:, g * SSD_STATE:(g + 1) * SSD_STATE]
        cg = cm[:, g * SSD_STATE:(g + 1) * SSD_STATE].astype(BF16)
        cb = _dot_nt(cg, bg.astype(BF16))
        heads_per_group = SSD_HEADS // SSD_GROUPS
        diag = []
        for pair in range(heads_per_group // 2):
            tick()
            blk = g * (heads_per_group // 2) + pair
            xp = xdt[:, blk * LANES:(blk + 1) * LANES]
            acc = None
            for half in range(2):
                h = 2 * blk + half
                seg = a_cum[:, h:h + 1] - a_cum_t[h:h + 1, :]
                dec = jnp.exp(jnp.where(causal, seg, -jnp.inf))
                m = (cb * dec).astype(BF16)
                keep = lo_half if half == 0 else jnp.logical_not(lo_half)
                part = _dot(m, jnp.where(keep, xp, 0.0).astype(BF16))
                acc = part if acc is None else acc + part
            diag.append(acc)
        y_diag = jnp.concatenate(diag, axis=1)
        prev = st_ref[g]
        y_off = _dot(cg, prev.astype(BF16)) * ea_x[:, g * gw:(g + 1) * gw]
        st_new = _dot(bg.T.astype(BF16), x2[:, g * gw:(g + 1) * gw])
        st_ref[g] = prev * cd_x[:, g * gw:(g + 1) * gw] + st_new
        y_parts.append(y_diag + y_off)

    tick()
    y = jnp.concatenate(y_parts, axis=1) + dsk_ref[...] * xs
    y = y * jax.nn.silu(z_ref[...])
    normed = []
    for g in range(SSD_GROUPS):
        yg = y[:, g * gw:(g + 1) * gw]
        normed.append(yg * lax.rsqrt(jnp.mean(yg * yg, axis=-1, keepdims=True) + EPS))
    return jnp.concatenate(normed, axis=1) * nw_ref[...]


def _lru_block(gate_ref, xl_ref, cw_ref, cb_ref, wa_ref, ba_ref, wx_ref, bx_ref, lam_ref,
               xpad_ref, h_ref, rows, tick):
    tick()
    xl = _causal_conv(xl_ref, xpad_ref, cw_ref, cb_ref, rows)
    tick()
    xlb = xl.astype(BF16)
    n_tiles = LRU_WIDTH // LRU_TILE

    def gate(w_ref, b_ref):
        parts = [_dot(xlb[:, t * LRU_TILE:(t + 1) * LRU_TILE], w_ref[t]) for t in range(n_tiles)]
        return jax.nn.sigmoid(jnp.concatenate(parts, axis=1) + b_ref[...])

    r = gate(wa_ref, ba_ref)
    tick()
    i = gate(wx_ref, bx_ref)
    tick()
    log_a = -LRU_C * r * jax.nn.softplus(-lam_ref[...])
    a = jnp.exp(log_a)
    mult = jnp.sqrt(jnp.maximum(1.0 - jnp.exp(2.0 * log_a), 0.0))
    tick()
    u = mult * (i * xl)

    groups = rows // SUBLANES
    a = a.reshape(groups, SUBLANES, LRU_WIDTH)
    u = u.reshape(groups, SUBLANES, LRU_WIDTH)
    sub = lax.broadcasted_iota(jnp.int32, (groups, SUBLANES, LRU_WIDTH), 1)
    k = 1
    while k < SUBLANES:
        valid = sub >= k
        a_sh = jnp.where(valid, pltpu.roll(a, k, 1), 1.0)
        u_sh = jnp.where(valid, pltpu.roll(u, k, 1), 0.0)
        u = a * u_sh + u
        a = a * a_sh
        k *= 2
        tick()
    carry = h_ref[...]
    hs = []
    for gi in range(groups):
        if gi % SUBLANES == 0:
            tick()
        hg = u[gi] + a[gi] * carry
        hs.append(hg)
        carry = jnp.broadcast_to(hg[SUBLANES - 1:SUBLANES, :], (SUBLANES, LRU_WIDTH))
    h_ref[...] = carry
    tick()
    h = jnp.concatenate(hs, axis=0)
    return h * jax.nn.gelu(gate_ref[...])


_Z0, _GATE0, _XL0, _XBC0, _DT0 = 0, D_MODEL, 2 * D_MODEL, 3 * D_MODEL, 3 * D_MODEL + SSD_CONV_DIM


def _mixer_kernel(x_ref, xn_ref, win_ref,
                  s_cw, s_cb, s_dtb, s_alog, s_dsk, s_nw, s_e,
                  l_cw, l_cb, l_wa, l_ba, l_wx, l_bx, l_lam,
                  wout_ref, g_ref, b_ref, o_ref,
                  pa_ref, pb_ref, s_xpad, s_st, l_xpad, l_h):
    r = MIX_ROWS
    first = jnp.logical_and(pl.program_id(0) == 0, pl.program_id(1) == 0)

    def projection_pieces(rows_ref, dst_ref):
        xb = rows_ref[...].astype(BF16)

        def piece(c0, width):
            dst_ref[:, c0:c0 + width] = _dot(xb, win_ref[:, c0:c0 + width])

        return [functools.partial(piece, c0, min(PROJ_PIECE, PROJ_COLS - c0))
                for c0 in range(0, PROJ_COLS, PROJ_PIECE)]

    def drain(pieces):
        while pieces:
            pieces.pop(0)()

    @pl.when(first)
    def _():
        drain(projection_pieces(x_ref.at[0:r], pa_ref))

    @pl.when(pl.program_id(1) == 0)
    def _():
        s_xpad[0:SUBLANES, :] = jnp.zeros((SUBLANES, SSD_CONV_DIM), F32)
        s_st[...] = jnp.zeros_like(s_st)
        l_xpad[0:SUBLANES, :] = jnp.zeros((SUBLANES, LRU_WIDTH), F32)
        l_h[...] = jnp.zeros_like(l_h)

    def mix(p_ref, row0, pieces):
        ticks = [0]

        def tick():
            ticks[0] += 1
            if pieces and ticks[0] % PROJ_TICKS_PER_PIECE == 0:
                pieces.pop(0)()

        ya = []
        for c0 in range(0, r, SSD_CHUNK):
            rows = pl.ds(c0, SSD_CHUNK)
            ya.append(_ssd_chunk(p_ref.at[rows, pl.ds(_Z0, SSD_INNER)],
                                 p_ref.at[rows, pl.ds(_XBC0, SSD_CONV_DIM)],
                                 p_ref.at[rows, pl.ds(_DT0, LANES)],
                                 s_cw, s_cb, s_dtb, s_alog, s_dsk, s_nw, s_e, s_xpad, s_st, tick))
        y_a = jnp.concatenate(ya, axis=0).astype(BF16)
        y_b = _lru_block(p_ref.at[:, pl.ds(_GATE0, LRU_WIDTH)], p_ref.at[:, pl.ds(_XL0, LRU_WIDTH)],
                         l_cw, l_cb, l_wa, l_ba, l_wx, l_bx, l_lam, l_xpad, l_h, r, tick).astype(BF16)
        drain(pieces)
        mixed = _dot(y_a, wout_ref[0:SSD_INNER, :]) + _dot(y_b, wout_ref[SSD_INNER:, :])
        o_ref[row0:row0 + r, :] = _layer_norm(ALPHA * x_ref[row0:row0 + r, :] + mixed,
                                              g_ref[...], b_ref[...])

    mix(pa_ref, 0, projection_pieces(x_ref.at[r:2 * r], pb_ref))
    mix(pb_ref, r, projection_pieces(xn_ref, pa_ref))


def _mixer_ln(x2d, bsz, seq, w_in, ssd_params, lru_params, w_out, g, b):
    r = MIX_ROWS
    t, d = x2d.shape
    steps = seq // (2 * r)
    last_blk = t // r - 1
    const = lambda a: pl.BlockSpec(a.shape, lambda bi, c: (0,) * a.ndim, pipeline_mode=pl.Buffered(1))
    params = [w_in, *ssd_params, *lru_params, w_out, g, b]
    return pl.pallas_call(
        _mixer_kernel,
        grid=(bsz, steps),
        in_specs=[pl.BlockSpec((2 * r, d), lambda bi, c: (bi * steps + c, 0)),
                  pl.BlockSpec((r, d), lambda bi, c: (jnp.minimum(2 * (bi * steps + c) + 2, last_blk), 0)),
                  *[const(a) for a in params]],
        out_specs=pl.BlockSpec((2 * r, d), lambda bi, c: (bi * steps + c, 0)),
        out_shape=jax.ShapeDtypeStruct((t, d), F32),
        scratch_shapes=[pltpu.VMEM((r, PROJ_COLS), F32), pltpu.VMEM((r, PROJ_COLS), F32),
                        pltpu.VMEM((SSD_CHUNK + SUBLANES, SSD_CONV_DIM), F32),
                        pltpu.VMEM((SSD_GROUPS, SSD_STATE, SSD_INNER // SSD_GROUPS), F32),
                        pltpu.VMEM((r + SUBLANES, LRU_WIDTH), F32),
                        pltpu.VMEM((SUBLANES, LRU_WIDTH), F32)],
        compiler_params=_cparams("arbitrary", "arbitrary"),
        name="mixer_ln",
    )(x2d, x2d, *params)


def _outproj_ln_kernel(*refs, n_in):
    a_refs, w_refs = refs[:n_in], refs[n_in:2 * n_in]
    res_ref, g_ref, b_ref, o_ref = refs[2 * n_in:]
    acc = ALPHA * res_ref[...]
    for a_ref, w_ref in zip(a_refs, w_refs):
        acc = acc + _dot(a_ref[...], w_ref[...])
    o_ref[...] = _layer_norm(acc, g_ref[...], b_ref[...])


def _outproj_ln(acts, weights, res, g, b, tm=512):
    t, d = res.shape
    n_in = len(acts)
    row = lambda width: pl.BlockSpec((tm, width), lambda i: (i, 0))
    return pl.pallas_call(
        functools.partial(_outproj_ln_kernel, n_in=n_in),
        grid=(t // tm,),
        in_specs=([row(a.shape[1]) for a in acts]
                  + [pl.BlockSpec(w.shape, lambda i: (0, 0)) for w in weights]
                  + [row(d), pl.BlockSpec((1, d), lambda i: (0, 0)),
                     pl.BlockSpec((1, d), lambda i: (0, 0))]),
        out_specs=row(d),
        out_shape=jax.ShapeDtypeStruct((t, d), F32),
        compiler_params=_cparams("arbitrary"),
        name="outproj_ln",
    )(*acts, *weights, res, g, b)


def _mlp_ln_kernel(x_ref, w1_ref, w2_ref, g_ref, b_ref, o_ref, acc_ref, xb_ref):
    j = pl.program_id(1)

    @pl.when(j == 0)
    def _():
        acc_ref[...] = jnp.zeros_like(acc_ref)
        xb_ref[...] = x_ref[...].astype(BF16)

    h = jnp.square(jnp.maximum(_dot(xb_ref[...], w1_ref[...]), 0.0))
    acc_ref[...] += _dot(h.astype(BF16), w2_ref[...])

    @pl.when(j == pl.num_programs(1) - 1)
    def _():
        o_ref[...] = _layer_norm(ALPHA * x_ref[...] + acc_ref[...], g_ref[...], b_ref[...])


def _mlp_ln(x2d, w1, w2, g, b, tm=512, tf=1024):
    t, d = x2d.shape
    ff = w1.shape[1]
    return pl.pallas_call(
        _mlp_ln_kernel,
        grid=(t // tm, ff // tf),
        in_specs=[pl.BlockSpec((tm, d), lambda i, j: (i, 0)),
                  pl.BlockSpec((d, tf), lambda i, j: (0, j)),
                  pl.BlockSpec((tf, d), lambda i, j: (j, 0)),
                  pl.BlockSpec((1, d), lambda i, j: (0, 0)),
                  pl.BlockSpec((1, d), lambda i, j: (0, 0))],
        out_specs=pl.BlockSpec((tm, d), lambda i, j: (i, 0)),
        out_shape=jax.ShapeDtypeStruct((t, d), F32),
        scratch_shapes=[pltpu.VMEM((tm, d), F32), pltpu.VMEM((tm, d), BF16)],
        compiler_params=_cparams("arbitrary", "arbitrary"),
        name="mlp_ln",
    )(x2d, w1, w2, g, b)


def _qkv_rope_kernel(x_ref, w_ref, pos_ref, freq_ref, s1_ref, s2_ref, qt_ref, k_ref, vt_ref):
    qk_w = DIFF_HEADS * DIFF_V_DIM
    acc = _dot(x_ref[...].astype(BF16), w_ref[...])
    ang = pos_ref[...].astype(F32) * freq_ref[...]
    cos, sin = jnp.cos(ang), jnp.sin(ang)
    reps = qk_w // LANES
    cos_w = jnp.concatenate([cos] * reps, axis=1)
    sin1_w = jnp.concatenate([sin * s1_ref[...]] * reps, axis=1)
    sin2_w = jnp.concatenate([sin * s2_ref[...]] * reps, axis=1)
    half = ROPE_DIM // 2

    def rope(t):
        return (t * cos_w + pltpu.roll(t, half, 1) * sin1_w
                + pltpu.roll(t, qk_w - half, 1) * sin2_w)

    rows = acc.shape[0]
    q = rope(acc[:, :qk_w]) * (DIFF_HEAD_DIM ** -0.5 * LOG2_E)
    qt_ref[...] = q.T.astype(qt_ref.dtype).reshape(DIFF_HEADS, DIFF_V_DIM, rows)
    k_ref[...] = rope(acc[:, qk_w:2 * qk_w]).astype(k_ref.dtype)
    vt_ref[...] = acc[:, 2 * qk_w:].T.astype(vt_ref.dtype).reshape(DIFF_HEADS, DIFF_V_DIM, rows)


def _qkv_rope(x2d, w, pos, freq, s1, s2, bsz, seq, blk):
    t, d = x2d.shape
    n = w.shape[1]
    nb = seq // blk
    vec = pl.BlockSpec((1, LANES), lambda i: (0, 0))
    tr_spec = pl.BlockSpec((None, DIFF_HEADS, None, DIFF_V_DIM, blk),
                           lambda i: (i // nb, 0, i % nb, 0, 0))
    tr_shape = jax.ShapeDtypeStruct((bsz, DIFF_HEADS, nb, DIFF_V_DIM, blk), BF16)
    return pl.pallas_call(
        _qkv_rope_kernel,
        grid=(t // blk,),
        in_specs=[pl.BlockSpec((blk, d), lambda i: (i, 0)),
                  pl.BlockSpec((d, n), lambda i: (0, 0)),
                  pl.BlockSpec((blk, 1), lambda i: (i, 0)),
                  vec, vec, vec],
        out_specs=[tr_spec, pl.BlockSpec((blk, n // 3), lambda i: (i, 0)), tr_spec],
        out_shape=[tr_shape, jax.ShapeDtypeStruct((t, n // 3), BF16), tr_shape],
        compiler_params=_cparams("arbitrary"),
        name="qkv_rope",
    )(x2d, w, pos, freq, s1, s2)


def _attn_kernel(qt_ref, k_ref, vt_ref, lq1_ref, lk1_ref, lq2_ref, lk2_ref, sw_ref, o_ref,
                 m_ref, acc_ref, sta_ref, stb_ref, *, blk, ratio, lambda_init):
    assert ratio % 2 == 0
    qb = ratio * blk
    qt = jnp.concatenate([qt_ref[r] for r in range(ratio)], axis=1).astype(F32)
    first = lax.broadcasted_iota(jnp.int32, (DIFF_V_DIM, qb), 0) < DIFF_HEAD_DIM
    qts = (jnp.where(first, qt, 0.0).astype(BF16), jnp.where(first, 0.0, qt).astype(BF16))

    m_ref[...] = jnp.full_like(m_ref, -jnp.inf)
    acc_ref[...] = jnp.zeros_like(acc_ref)
    ones_rows = jnp.ones((BF16_ROWS, blk), BF16)

    def scores(j, st_ref, col0=0):
        start = pl.multiple_of(j * blk, blk)
        k = k_ref[pl.ds(start, blk), :]
        for comp in range(2):
            st_ref[comp, :, col0:] = _dot(k, qts[comp][:, col0:])

    def accumulate(j, st_ref, col0=0, masked=False):
        vt = jnp.concatenate([vt_ref[j], ones_rows], axis=0)
        if masked:
            key = lax.broadcasted_iota(jnp.int32, (blk, qb - col0), 0)
            qry = lax.broadcasted_iota(jnp.int32, (blk, qb - col0), 1)
            keep = key <= qry
        for comp in range(2):
            st = st_ref[comp, :, col0:]
            if masked:
                st = jnp.where(keep, st, -jnp.inf)
            m_prev = m_ref[comp, :, col0:]
            m_new = jnp.maximum(m_prev, jnp.max(st, axis=0, keepdims=True))
            alpha = jnp.exp2(m_prev - m_new)
            p = jnp.exp2(st - m_new)
            acc_ref[comp, :, col0:] = alpha * acc_ref[comp, :, col0:] + _dot(vt, p.astype(BF16))
            m_ref[comp, :, col0:] = m_new

    n_full = ratio * pl.program_id(2)

    def pair(u, carry):
        j = 2 * u
        scores(j + 1, stb_ref)
        accumulate(j, sta_ref)
        scores(j + 2, sta_ref)
        accumulate(j + 1, stb_ref)
        return carry

    scores(0, sta_ref)
    lax.fori_loop(0, n_full // 2, pair, 0)
    slots = (sta_ref, stb_ref)
    for s in range(ratio):
        if s + 1 < ratio:
            scores(n_full + s + 1, slots[(s + 1) % 2], (s + 1) * blk)
        accumulate(n_full + s, slots[s % 2], s * blk, masked=True)

    lam = (jnp.exp(jnp.sum(lq1_ref[...] * lk1_ref[...], axis=-1, keepdims=True))
           - jnp.exp(jnp.sum(lq2_ref[...] * lk2_ref[...], axis=-1, keepdims=True)) + lambda_init)
    hd = DIFF_V_DIM
    outs = [acc_ref[comp, :hd, :] / acc_ref[comp, hd:hd + 1, :] for comp in range(2)]
    ot = outs[0] - lam * outs[1]
    ot = ot * lax.rsqrt(jnp.mean(ot * ot, axis=0, keepdims=True) + EPS)
    ot = ot * sw_ref[...] * (1.0 - lambda_init)
    o_ref[...] = ot.T.astype(o_ref.dtype)


def _attn(qt, k, vt, lq1, lk1, lq2, lk2, sw_col, lambda_init, ratio=ATTN_Q_RATIO):
    bsz, heads, nb, hd, blk = qt.shape
    seq = nb * blk
    qb = ratio * blk
    nq = nb // ratio
    vec = lambda w: pl.BlockSpec((1, w), lambda b, h, i: (0, 0))
    return pl.pallas_call(
        functools.partial(_attn_kernel, blk=blk, ratio=ratio, lambda_init=lambda_init),
        grid=(bsz, heads, nq),
        in_specs=[pl.BlockSpec((None, None, ratio, hd, blk), lambda b, h, i: (b, h, i, 0, 0)),
                  pl.BlockSpec((seq, hd), lambda b, h, i: (b, h)),
                  pl.BlockSpec((None, None, nb, hd, blk), lambda b, h, i: (b, h, 0, 0, 0)),
                  vec(DIFF_HEAD_DIM), vec(DIFF_HEAD_DIM), vec(DIFF_HEAD_DIM), vec(DIFF_HEAD_DIM),
                  pl.BlockSpec((hd, 1), lambda b, h, i: (0, 0))],
        out_specs=pl.BlockSpec((qb, hd), lambda b, h, i: (b * nq + i, h)),
        out_shape=jax.ShapeDtypeStruct((bsz * seq, heads * hd), BF16),
        scratch_shapes=[pltpu.VMEM((2, 1, qb), F32),
                        pltpu.VMEM((2, hd + BF16_ROWS, qb), F32),
                        pltpu.VMEM((2, blk, qb), F32), pltpu.VMEM((2, blk, qb), F32)],
        compiler_params=_cparams("arbitrary", "arbitrary", "arbitrary"),
        name="diff_attn",
    )(qt, k, vt, lq1, lk1, lq2, lk2, sw_col)


def _block_diag_tiles(w):
    per = LRU_TILE // LRU_BLOCK
    w4 = w.reshape(LRU_HEADS // per, per, LRU_BLOCK, LRU_BLOCK)
    bd = jnp.einsum('tgij,gh->tgihj', w4, jnp.eye(per, dtype=w.dtype))
    return bd.reshape(LRU_HEADS // per, LRU_TILE, LRU_TILE)


def _pad_lanes(v):
    return jnp.pad(v, (0, LANES - v.shape[0]))[None, :]


def kernel(x, positions, ssm_w_in, ssm_conv_w, ssm_conv_b, ssm_dt_bias, ssm_a_log, ssm_d, ssm_norm_w, lru_conv_w, lru_conv_b, lru_w_a, lru_b_a, lru_w_x, lru_b_x, lru_lambda, mix_w_out, attn_w_qkv, attn_lq1, attn_lk1, attn_lq2, attn_lk2, attn_subln_w, attn_w_out, ln1_g, ln1_b, ff_w1, ff_w2, ln2_g, ln2_b):
    bsz, seq, d = x.shape
    h = x.reshape(bsz * seq, d)
    row = lambda v: v[None, :]

    head_of_lane = jnp.arange(SSD_INNER) // SSD_HEAD_DIM
    expand = (jnp.arange(LANES)[:, None] == head_of_lane[None, :]).astype(BF16)

    inv_freq = ROPE_THETA ** (-jnp.arange(0, ROPE_DIM, 2, dtype=F32) / ROPE_DIM)
    lane = jnp.arange(LANES)
    in_head = lane % DIFF_HEAD_DIM
    freq = jnp.where(in_head < ROPE_DIM, inv_freq[lane % (ROPE_DIM // 2)], 0.0)[None, :]
    sin_up = ((in_head >= ROPE_DIM // 2) & (in_head < ROPE_DIM)).astype(F32)[None, :]
    sin_dn = -(in_head < ROPE_DIM // 2).astype(F32)[None, :]
    pos = positions.reshape(bsz * seq, 1)

    for layer in range(DEPTH):
        i = layer // 2
        if layer % 2 == 0:
            w_in = ssm_w_in[i]
            s0, s1, s2, s3 = (SSD_INNER, SSD_INNER + SSD_CONV_DIM,
                              SSD_INNER + SSD_CONV_DIM + SSD_HEADS,
                              SSD_INNER + SSD_CONV_DIM + SSD_HEADS + LRU_WIDTH)
            w_perm = jnp.concatenate(
                [w_in[:, :s0], w_in[:, s2:s3], w_in[:, s3:], w_in[:, s0:s1], w_in[:, s1:s2],
                 jnp.zeros((d, LANES - SSD_HEADS), w_in.dtype)], axis=1).astype(BF16)
            ssd_params = (ssm_conv_w[i], row(ssm_conv_b[i]),
                          _pad_lanes(ssm_dt_bias[i]), _pad_lanes(ssm_a_log[i]),
                          row(jnp.repeat(ssm_d[i], SSD_HEAD_DIM)), row(ssm_norm_w[i]), expand)
            lru_params = (lru_conv_w[i], row(lru_conv_b[i]),
                          _block_diag_tiles(lru_w_a[i]).astype(BF16), row(lru_b_a[i].reshape(-1)),
                          _block_diag_tiles(lru_w_x[i]).astype(BF16), row(lru_b_x[i].reshape(-1)),
                          row(lru_lambda[i]))
            h = _mixer_ln(h, bsz, seq, w_perm, ssd_params, lru_params, mix_w_out[i].astype(BF16),
                          row(ln1_g[layer]), row(ln1_b[layer]))
        else:
            lambda_init = 0.8 - 0.6 * math.exp(-0.3 * layer)
            qt, k, vt = _qkv_rope(h, attn_w_qkv[i].astype(BF16), pos, freq, sin_up, sin_dn,
                                  bsz, seq, ATTN_BLOCK)
            o = _attn(qt, k, vt, row(attn_lq1[i]), row(attn_lk1[i]), row(attn_lq2[i]),
                      row(attn_lk2[i]), attn_subln_w[i][:, None], lambda_init)
            h = _outproj_ln([o], [attn_w_out[i].astype(BF16)], h,
                            row(ln1_g[layer]), row(ln1_b[layer]))
        h = _mlp_ln(h, ff_w1[layer].astype(BF16), ff_w2[layer].astype(BF16),
                    row(ln2_g[layer]), row(ln2_b[layer]))
    return h.reshape(bsz, seq, d)
```

```python
import functools
import math

import jax
import jax.numpy as jnp
from jax import lax
from jax.experimental import pallas as pl
from jax.experimental.pallas import tpu as pltpu

F32 = jnp.float32
BF16 = jnp.bfloat16

SUBLANES = 8
LANES = 128
BF16_ROWS = 2 * SUBLANES
LOG2_E = math.log2(math.e)

D_MODEL = 1024
DEPTH = 2
SSD_HEADS = 16
SSD_HEAD_DIM = 64
SSD_INNER = SSD_HEADS * SSD_HEAD_DIM
SSD_GROUPS = 2
SSD_STATE = 128
SSD_CHUNK = 128
SSD_CONV_DIM = SSD_INNER + 2 * SSD_GROUPS * SSD_STATE
CONV_WIDTH = 4
LRU_HEADS = 16
LRU_WIDTH = D_MODEL
LRU_BLOCK = LRU_WIDTH // LRU_HEADS
LRU_C = 8.0
LRU_TILE = 256
DIFF_HEADS = 8
DIFF_HEAD_DIM = 64
DIFF_V_DIM = 2 * DIFF_HEAD_DIM
ROPE_DIM = DIFF_HEAD_DIM // 4
ROPE_THETA = 500000.0
ATTN_BLOCK = 512
ATTN_Q_RATIO = 4
ATTN_Q_PIECE = 256
D_FF = 4 * D_MODEL
ALPHA = (2 * DEPTH) ** 0.25
EPS = 1e-5

PROJ_COLS = 3 * D_MODEL + SSD_CONV_DIM + LANES
MIX_ROWS = 256
PROJ_PIECE = 256
PROJ_TICKS_PER_PIECE = 2

VMEM_LIMIT = 56 * 1024 * 1024


def _cparams(*sem):
    return pltpu.CompilerParams(dimension_semantics=sem, vmem_limit_bytes=VMEM_LIMIT)


def _layer_norm(v, g, b):
    mu = jnp.mean(v, axis=-1, keepdims=True)
    d = v - mu
    var = jnp.mean(d * d, axis=-1, keepdims=True)
    return d * lax.rsqrt(var + EPS) * g + b


def _dot(a, b):
    return jnp.dot(a, b, preferred_element_type=F32)


def _dot_nt(a, b):
    return lax.dot_general(a, b, (((1,), (1,)), ((), ())), preferred_element_type=F32)


def _causal_conv(x_ref, tail_ref, cw_ref, cb_ref, rows):
    width = x_ref.shape[-1]
    groups = rows // SUBLANES
    x = x_ref[...].reshape(groups, SUBLANES, width)
    ext = jnp.concatenate([tail_ref[...][None], x], axis=0)
    sub = lax.broadcasted_iota(jnp.int32, (groups, SUBLANES, width), 1)
    out = cb_ref[...] + cw_ref[CONV_WIDTH - 1:CONV_WIDTH, :] * x
    for s in range(1, CONV_WIDTH):
        rot = pltpu.roll(ext, s, 1)
        shifted = jnp.where(sub >= s, rot[1:], rot[:-1])
        out = out + cw_ref[CONV_WIDTH - 1 - s:CONV_WIDTH - s, :] * shifted
    tail_ref[...] = x[groups - 1]
    return out.reshape(rows, width)


def _expand_heads(v, e):
    hi = v.astype(BF16)
    lo = (v - hi.astype(F32)).astype(BF16)
    return _dot(hi, e) + _dot(lo, e)


def _ssd_chunk(z_ref, xbc_ref, dt_ref, cw_ref, cb_ref, dtb_ref, alog_ref, dsk_ref, nw_ref, e_ref,
               xpad_ref, st_ref, tick):
    l = SSD_CHUNK
    gw = SSD_INNER // SSD_GROUPS
    tick()
    xbc = jax.nn.silu(_causal_conv(xbc_ref, xpad_ref, cw_ref, cb_ref, l))
    tick()
    xs = xbc[:, :SSD_INNER]
    bm = xbc[:, SSD_INNER:SSD_INNER + SSD_GROUPS * SSD_STATE]
    cm = xbc[:, SSD_INNER + SSD_GROUPS * SSD_STATE:]

    dt = jax.nn.softplus(dt_ref[...] + dtb_ref[...])
    adt = dt * (-jnp.exp(alog_ref[...]))
    row = lax.broadcasted_iota(jnp.int32, (l, l), 0)
    col = lax.broadcasted_iota(jnp.int32, (l, l), 1)
    causal = row >= col
    a_cum = jnp.dot(causal.astype(F32), adt, precision=lax.Precision.HIGHEST,
                    preferred_element_type=F32)
    a_cum_t = a_cum.T
    a_last = a_cum[l - 1:l, :]
    ea = jnp.exp(a_cum)
    stacked = jnp.concatenate([dt, ea, dt * jnp.exp(a_last - a_cum)], axis=0)
    ex = _expand_heads(stacked, e_ref[...])
    dt_x, ea_x, ds_x = ex[:l], ex[l:2 * l], ex[2 * l:]
    xdt = xs * dt_x
    x2 = (xs * ds_x).astype(BF16)
    cd_x = ea_x[l - 1:l, :]

    lane = lax.broadcasted_iota(jnp.int32, (l, LANES), 1)
    lo_half = lane < SSD_HEAD_DIM
    y_parts = []
    for g in range(SSD_GROUPS):
        bg = bm[:, g * SSD_STATE:(g + 1) * SSD_STATE]
        cg = cm[:, g * SSD_STATE:(g + 1) * SSD_STATE].astype(BF16)
        cb = _dot_nt(cg, bg.astype(BF16))
        heads_per_group = SSD_HEADS // SSD_GROUPS
        diag = []
        for pair in range(heads_per_group // 2):
            tick()
            blk = g * (heads_per_group // 2) + pair
            xp = xdt[:, blk * LANES:(blk + 1) * LANES]
            acc = None
            for half in range(2):
                h = 2 * blk + half
                seg = a_cum[:, h:h + 1] - a_cum_t[h:h + 1, :]
                dec = jnp.exp(jnp.where(causal, seg, -jnp.inf))
                m = (cb * dec).astype(BF16)
                keep = lo_half if half == 0 else jnp.logical_not(lo_half)
                part = _dot(m, jnp.where(keep, xp, 0.0).astype(BF16))
                acc = part if acc is None else acc + part
            diag.append(acc)
        y_diag = jnp.concatenate(diag, axis=1)
        prev = st_ref[g]
        y_off = _dot(cg, prev.astype(BF16)) * ea_x[:, g * gw:(g + 1) * gw]
        st_new = _dot(bg.T.astype(BF16), x2[:, g * gw:(g + 1) * gw])
        st_ref[g] = prev * cd_x[:, g * gw:(g + 1) * gw] + st_new
        y_parts.append(y_diag + y_off)

    tick()
    y = jnp.concatenate(y_parts, axis=1) + dsk_ref[...] * xs
    y = y * jax.nn.silu(z_ref[...])
    normed = []
    for g in range(SSD_GROUPS):
        yg = y[:, g * gw:(g + 1) * gw]
        normed.append(yg * lax.rsqrt(jnp.mean(yg * yg, axis=-1, keepdims=True) + EPS))
    return jnp.concatenate(normed, axis=1) * nw_ref[...]


def _lru_block(gate_ref, xl_ref, cw_ref, cb_ref, wa_ref, ba_ref, wx_ref, bx_ref, lam_ref,
               xpad_ref, h_ref, rows, tick):
    tick()
    xl = _causal_conv(xl_ref, xpad_ref, cw_ref, cb_ref, rows)
    tick()
    xlb = xl.astype(BF16)
    n_tiles = LRU_WIDTH // LRU_TILE

    def gate(w_ref, b_ref):
        parts = [_dot(xlb[:, t * LRU_TILE:(t + 1) * LRU_TILE], w_ref[t]) for t in range(n_tiles)]
        return jax.nn.sigmoid(jnp.concatenate(parts, axis=1) + b_ref[...])

    r = gate(wa_ref, ba_ref)
    tick()
    i = gate(wx_ref, bx_ref)
    tick()
    log_a = -LRU_C * r * jax.nn.softplus(-lam_ref[...])
    a = jnp.exp(log_a)
    mult = jnp.sqrt(jnp.maximum(1.0 - jnp.exp(2.0 * log_a), 0.0))
    tick()
    u = mult * (i * xl)

    groups = rows // SUBLANES
    a = a.reshape(groups, SUBLANES, LRU_WIDTH)
    u = u.reshape(groups, SUBLANES, LRU_WIDTH)
    sub = lax.broadcasted_iota(jnp.int32, (groups, SUBLANES, LRU_WIDTH), 1)
    k = 1
    while k < SUBLANES:
        valid = sub >= k
        a_sh = jnp.where(valid, pltpu.roll(a, k, 1), 1.0)
        u_sh = jnp.where(valid, pltpu.roll(u, k, 1), 0.0)
        u = a * u_sh + u
        a = a * a_sh
        k *= 2
        tick()
    carry = h_ref[...]
    hs = []
    for gi in range(groups):
        if gi % SUBLANES == 0:
            tick()
        hg = u[gi] + a[gi] * carry
        hs.append(hg)
        carry = jnp.broadcast_to(hg[SUBLANES - 1:SUBLANES, :], (SUBLANES, LRU_WIDTH))
    h_ref[...] = carry
    tick()
    h = jnp.concatenate(hs, axis=0)
    return h * jax.nn.gelu(gate_ref[...])


_Z0, _GATE0, _XL0, _XBC0, _DT0 = 0, D_MODEL, 2 * D_MODEL, 3 * D_MODEL, 3 * D_MODEL + SSD_CONV_DIM


def _mixer_kernel(x_ref, xn_ref, win_ref,
                  s_cw, s_cb, s_dtb, s_alog, s_dsk, s_nw, s_e,
                  l_cw, l_cb, l_wa, l_ba, l_wx, l_bx, l_lam,
                  wout_ref, g_ref, b_ref, o_ref,
                  pa_ref, pb_ref, s_xpad, s_st, l_xpad, l_h):
    r = MIX_ROWS
    first = jnp.logical_and(pl.program_id(0) == 0, pl.program_id(1) == 0)

    def projection_pieces(rows_ref, dst_ref):
        xb = rows_ref[...].astype(BF16)

        def piece(c0, width):
            dst_ref[:, c0:c0 + width] = _dot(xb, win_ref[:, c0:c0 + width])

        return [functools.partial(piece, c0, min(PROJ_PIECE, PROJ_COLS - c0))
                for c0 in range(0, PROJ_COLS, PROJ_PIECE)]

    def drain(pieces):
        while pieces:
            pieces.pop(0)()

    @pl.when(first)
    def _():
        drain(projection_pieces(x_ref.at[0:r], pa_ref))

    @pl.when(pl.program_id(1) == 0)
    def _():
        s_xpad[...] = jnp.zeros_like(s_xpad)
        s_st[...] = jnp.zeros_like(s_st)
        l_xpad[...] = jnp.zeros_like(l_xpad)
        l_h[...] = jnp.zeros_like(l_h)

    def mix(p_ref, row0, pieces):
        ticks = [0]

        def tick():
            ticks[0] += 1
            if pieces and ticks[0] % PROJ_TICKS_PER_PIECE == 0:
                pieces.pop(0)()

        ya = []
        for c0 in range(0, r, SSD_CHUNK):
            rows = pl.ds(c0, SSD_CHUNK)
            ya.append(_ssd_chunk(p_ref.at[rows, pl.ds(_Z0, SSD_INNER)],
                                 p_ref.at[rows, pl.ds(_XBC0, SSD_CONV_DIM)],
                                 p_ref.at[rows, pl.ds(_DT0, LANES)],
                                 s_cw, s_cb, s_dtb, s_alog, s_dsk, s_nw, s_e, s_xpad, s_st, tick))
        y_a = jnp.concatenate(ya, axis=0).astype(BF16)
        y_b = _lru_block(p_ref.at[:, pl.ds(_GATE0, LRU_WIDTH)], p_ref.at[:, pl.ds(_XL0, LRU_WIDTH)],
                         l_cw, l_cb, l_wa, l_ba, l_wx, l_bx, l_lam, l_xpad, l_h, r, tick).astype(BF16)
        drain(pieces)
        mixed = _dot(y_a, wout_ref[0:SSD_INNER, :]) + _dot(y_b, wout_ref[SSD_INNER:, :])
        o_ref[row0:row0 + r, :] = _layer_norm(ALPHA * x_ref[row0:row0 + r, :] + mixed,
                                              g_ref[...], b_ref[...])

    mix(pa_ref, 0, projection_pieces(x_ref.at[r:2 * r], pb_ref))
    mix(pb_ref, r, projection_pieces(xn_ref, pa_ref))


def _mixer_ln(x2d, bsz, seq, w_in, ssd_params, lru_params, w_out, g, b):
    r = MIX_ROWS
    t, d = x2d.shape
    steps = seq // (2 * r)
    last_blk = t // r - 1
    const = lambda a: pl.BlockSpec(a.shape, lambda bi, c: (0,) * a.ndim, pipeline_mode=pl.Buffered(1))
    params = [w_in, *ssd_params, *lru_params, w_out, g, b]
    return pl.pallas_call(
        _mixer_kernel,
        grid=(bsz, steps),
        in_specs=[pl.BlockSpec((2 * r, d), lambda bi, c: (bi * steps + c, 0)),
                  pl.BlockSpec((r, d), lambda bi, c: (jnp.minimum(2 * (bi * steps + c) + 2, last_blk), 0)),
                  *[const(a) for a in params]],
        out_specs=pl.BlockSpec((2 * r, d), lambda bi, c: (bi * steps + c, 0)),
        out_shape=jax.ShapeDtypeStruct((t, d), F32),
        scratch_shapes=[pltpu.VMEM((r, PROJ_COLS), F32), pltpu.VMEM((r, PROJ_COLS), F32),
                        pltpu.VMEM((SUBLANES, SSD_CONV_DIM), F32),
                        pltpu.VMEM((SSD_GROUPS, SSD_STATE, SSD_INNER // SSD_GROUPS), F32),
                        pltpu.VMEM((SUBLANES, LRU_WIDTH), F32),
                        pltpu.VMEM((SUBLANES, LRU_WIDTH), F32)],
        compiler_params=_cparams("arbitrary", "arbitrary"),
        name="mixer_ln",
    )(x2d, x2d, *params)


def _outproj_ln_kernel(*refs, n_in):
    a_refs, w_refs = refs[:n_in], refs[n_in:2 * n_in]
    res_ref, g_ref, b_ref, o_ref = refs[2 * n_in:]
    acc = ALPHA * res_ref[...]
    for a_ref, w_ref in zip(a_refs, w_refs):
        acc = acc + _dot(a_ref[...], w_ref[...])
    o_ref[...] = _layer_norm(acc, g_ref[...], b_ref[...])


def _outproj_ln(acts, weights, res, g, b, tm=512):
    t, d = res.shape
    n_in = len(acts)
    row = lambda width: pl.BlockSpec((tm, width), lambda i: (i, 0))
    return pl.pallas_call(
        functools.partial(_outproj_ln_kernel, n_in=n_in),
        grid=(t // tm,),
        in_specs=([row(a.shape[1]) for a in acts]
                  + [pl.BlockSpec(w.shape, lambda i: (0, 0)) for w in weights]
                  + [row(d), pl.BlockSpec((1, d), lambda i: (0, 0)),
                     pl.BlockSpec((1, d), lambda i: (0, 0))]),
        out_specs=row(d),
        out_shape=jax.ShapeDtypeStruct((t, d), F32),
        compiler_params=_cparams("arbitrary"),
        name="outproj_ln",
    )(*acts, *weights, res, g, b)


def _mlp_ln_kernel(x_ref, w1_ref, w2_ref, g_ref, b_ref, o_ref, acc_ref, xb_ref):
    j = pl.program_id(1)

    @pl.when(j == 0)
    def _():
        acc_ref[...] = jnp.zeros_like(acc_ref)
        xb_ref[...] = x_ref[...].astype(BF16)

    h = jnp.square(jnp.maximum(_dot(xb_ref[...], w1_ref[...]), 0.0))
    acc_ref[...] += _dot(h.astype(BF16), w2_ref[...])

    @pl.when(j == pl.num_programs(1) - 1)
    def _():
        o_ref[...] = _layer_norm(ALPHA * x_ref[...] + acc_ref[...], g_ref[...], b_ref[...])


def _mlp_ln(x2d, w1, w2, g, b, tm=512, tf=1024):
    t, d = x2d.shape
    ff = w1.shape[1]
    return pl.pallas_call(
        _mlp_ln_kernel,
        grid=(t // tm, ff // tf),
        in_specs=[pl.BlockSpec((tm, d), lambda i, j: (i, 0)),
                  pl.BlockSpec((d, tf), lambda i, j: (0, j)),
                  pl.BlockSpec((tf, d), lambda i, j: (j, 0)),
                  pl.BlockSpec((1, d), lambda i, j: (0, 0)),
                  pl.BlockSpec((1, d), lambda i, j: (0, 0))],
        out_specs=pl.BlockSpec((tm, d), lambda i, j: (i, 0)),
        out_shape=jax.ShapeDtypeStruct((t, d), F32),
        scratch_shapes=[pltpu.VMEM((tm, d), F32), pltpu.VMEM((tm, d), BF16)],
        compiler_params=_cparams("arbitrary", "arbitrary"),
        name="mlp_ln",
    )(x2d, w1, w2, g, b)


def _qkv_rope_kernel(x_ref, w_ref, pos_ref, freq_ref, s1_ref, s2_ref, qt_ref, k_ref, vt_ref):
    qk_w = DIFF_HEADS * DIFF_V_DIM
    acc = _dot(x_ref[...].astype(BF16), w_ref[...])
    ang = pos_ref[...].astype(F32) * freq_ref[...]
    cos, sin = jnp.cos(ang), jnp.sin(ang)
    reps = qk_w // LANES
    cos_w = jnp.concatenate([cos] * reps, axis=1)
    sin1_w = jnp.concatenate([sin * s1_ref[...]] * reps, axis=1)
    sin2_w = jnp.concatenate([sin * s2_ref[...]] * reps, axis=1)
    half = ROPE_DIM // 2

    def rope(t):
        return (t * cos_w + pltpu.roll(t, half, 1) * sin1_w
                + pltpu.roll(t, qk_w - half, 1) * sin2_w)

    rows = acc.shape[0]
    q = rope(acc[:, :qk_w]) * (DIFF_HEAD_DIM ** -0.5 * LOG2_E)
    qt_ref[...] = q.T.astype(qt_ref.dtype).reshape(DIFF_HEADS, DIFF_V_DIM, rows)
    k_ref[...] = rope(acc[:, qk_w:2 * qk_w]).astype(k_ref.dtype)
    vt_ref[...] = acc[:, 2 * qk_w:].T.astype(vt_ref.dtype).reshape(DIFF_HEADS, DIFF_V_DIM, rows)


def _qkv_rope(x2d, w, pos, freq, s1, s2, bsz, seq, blk):
    t, d = x2d.shape
    n = w.shape[1]
    nb = seq // blk
    vec = pl.BlockSpec((1, LANES), lambda i: (0, 0))
    tr_spec = pl.BlockSpec((None, DIFF_HEADS, None, DIFF_V_DIM, blk),
                           lambda i: (i // nb, 0, i % nb, 0, 0))
    tr_shape = jax.ShapeDtypeStruct((bsz, DIFF_HEADS, nb, DIFF_V_DIM, blk), BF16)
    return pl.pallas_call(
        _qkv_rope_kernel,
        grid=(t // blk,),
        in_specs=[pl.BlockSpec((blk, d), lambda i: (i, 0)),
                  pl.BlockSpec((d, n), lambda i: (0, 0)),
                  pl.BlockSpec((blk, 1), lambda i: (i, 0)),
                  vec, vec, vec],
        out_specs=[tr_spec, pl.BlockSpec((blk, n // 3), lambda i: (i, 0)), tr_spec],
        out_shape=[tr_shape, jax.ShapeDtypeStruct((t, n // 3), BF16), tr_shape],
        compiler_params=_cparams("arbitrary"),
        name="qkv_rope",
    )(x2d, w, pos, freq, s1, s2)


def _attn_kernel(qt_ref, k_ref, vt_ref, lq1_ref, lk1_ref, lq2_ref, lk2_ref, sw_ref, o_ref,
                 m_ref, acc_ref, sta_ref, stb_ref, *, blk, ratio, lambda_init):
    assert ratio % 2 == 0
    qb = ratio * blk
    qt = jnp.concatenate([qt_ref[r] for r in range(ratio)], axis=1).astype(F32)
    first = lax.broadcasted_iota(jnp.int32, (DIFF_V_DIM, qb), 0) < DIFF_HEAD_DIM
    qts = (jnp.where(first, qt, 0.0).astype(BF16), jnp.where(first, 0.0, qt).astype(BF16))

    m_ref[...] = jnp.full_like(m_ref, -jnp.inf)
    acc_ref[...] = jnp.zeros_like(acc_ref)
    ones_rows = jnp.ones((BF16_ROWS, blk), BF16)

    w = ATTN_Q_PIECE

    def scores(j, st_ref, c0):
        start = pl.multiple_of(j * blk, blk)
        k = k_ref[pl.ds(start, blk), :]
        for comp in range(2):
            st_ref[comp, :, c0:c0 + w] = _dot(k, qts[comp][:, c0:c0 + w])

    def accumulate(j, st_ref, c0, diag_off=None):
        vt = jnp.concatenate([vt_ref[j], ones_rows], axis=0)
        if diag_off is not None:
            key = lax.broadcasted_iota(jnp.int32, (blk, w), 0)
            qry = lax.broadcasted_iota(jnp.int32, (blk, w), 1) + diag_off
            keep = key <= qry
        for comp in range(2):
            st = st_ref[comp, :, c0:c0 + w]
            if diag_off is not None:
                st = jnp.where(keep, st, -jnp.inf)
            m_prev = m_ref[comp, :, c0:c0 + w]
            m_new = jnp.maximum(m_prev, jnp.max(st, axis=0, keepdims=True))
            alpha = jnp.exp2(m_prev - m_new)
            p = jnp.exp2(st - m_new)
            acc_ref[comp, :, c0:c0 + w] = (alpha * acc_ref[comp, :, c0:c0 + w]
                                           + _dot(vt, p.astype(BF16)))
            m_ref[comp, :, c0:c0 + w] = m_new

    n_full = ratio * pl.program_id(2)
    slots = (sta_ref, stb_ref)

    def pair(u, carry):
        for half in range(2):
            j = 2 * u + half
            for c0 in range(0, qb, w):
                scores(j + 1, slots[1 - half], c0)
                accumulate(j, slots[half], c0)
        return carry

    for c0 in range(0, qb, w):
        scores(0, sta_ref, c0)
    lax.fori_loop(0, n_full // 2, pair, 0)
    for s in range(ratio):
        for c0 in range(s * blk, qb, w):
            if s + 1 < ratio and c0 >= (s + 1) * blk:
                scores(n_full + s + 1, slots[(s + 1) % 2], c0)
            accumulate(n_full + s, slots[s % 2], c0,
                       diag_off=c0 - s * blk if c0 < (s + 1) * blk else None)

    lam = (jnp.exp(jnp.sum(lq1_ref[...] * lk1_ref[...], axis=-1, keepdims=True))
           - jnp.exp(jnp.sum(lq2_ref[...] * lk2_ref[...], axis=-1, keepdims=True)) + lambda_init)
    hd = DIFF_V_DIM
    outs = [acc_ref[comp, :hd, :] / acc_ref[comp, hd:hd + 1, :] for comp in range(2)]
    ot = outs[0] - lam * outs[1]
    ot = ot * lax.rsqrt(jnp.mean(ot * ot, axis=0, keepdims=True) + EPS)
    ot = ot * sw_ref[...] * (1.0 - lambda_init)
    o_ref[...] = ot.T.astype(o_ref.dtype)


def _attn(qt, k, vt, lq1, lk1, lq2, lk2, sw_col, lambda_init, ratio=ATTN_Q_RATIO):
    bsz, heads, nb, hd, blk = qt.shape
    seq = nb * blk
    qb = ratio * blk
    nq = nb // ratio
    vec = lambda w: pl.BlockSpec((1, w), lambda b, h, i: (0, 0))
    return pl.pallas_call(
        functools.partial(_attn_kernel, blk=blk, ratio=ratio, lambda_init=lambda_init),
        grid=(bsz, heads, nq),
        in_specs=[pl.BlockSpec((None, None, ratio, hd, blk), lambda b, h, i: (b, h, i, 0, 0)),
                  pl.BlockSpec((seq, hd), lambda b, h, i: (b, h)),
                  pl.BlockSpec((None, None, nb, hd, blk), lambda b, h, i: (b, h, 0, 0, 0)),
                  vec(DIFF_HEAD_DIM), vec(DIFF_HEAD_DIM), vec(DIFF_HEAD_DIM), vec(DIFF_HEAD_DIM),
                  pl.BlockSpec((hd, 1), lambda b, h, i: (0, 0))],
        out_specs=pl.BlockSpec((qb, hd), lambda b, h, i: (b * nq + i, h)),
        out_shape=jax.ShapeDtypeStruct((bsz * seq, heads * hd), BF16),
        scratch_shapes=[pltpu.VMEM((2, 1, qb), F32),
                        pltpu.VMEM((2, hd + BF16_ROWS, qb), F32),
                        pltpu.VMEM((2, blk, qb), F32), pltpu.VMEM((2, blk, qb), F32)],
        compiler_params=_cparams("arbitrary", "arbitrary", "arbitrary"),
        name="diff_attn",
    )(qt, k, vt, lq1, lk1, lq2, lk2, sw_col)


def _block_diag_tiles(w):
    per = LRU_TILE // LRU_BLOCK
    w4 = w.reshape(LRU_HEADS // per, per, LRU_BLOCK, LRU_BLOCK)
    bd = jnp.einsum('tgij,gh->tgihj', w4, jnp.eye(per, dtype=w.dtype))
    return bd.reshape(LRU_HEADS // per, LRU_TILE, LRU_TILE)


def _pad_lanes(v):
    return jnp.pad(v, (0, LANES - v.shape[0]))[None, :]


def kernel(x, positions, ssm_w_in, ssm_conv_w, ssm_conv_b, ssm_dt_bias, ssm_a_log, ssm_d, ssm_norm_w, lru_conv_w, lru_conv_b, lru_w_a, lru_b_a, lru_w_x, lru_b_x, lru_lambda, mix_w_out, attn_w_qkv, attn_lq1, attn_lk1, attn_lq2, attn_lk2, attn_subln_w, attn_w_out, ln1_g, ln1_b, ff_w1, ff_w2, ln2_g, ln2_b):
    bsz, seq, d = x.shape
    h = x.reshape(bsz * seq, d)
    row = lambda v: v[None, :]

    head_of_lane = jnp.arange(SSD_INNER) // SSD_HEAD_DIM
    expand = (jnp.arange(LANES)[:, None] == head_of_lane[None, :]).astype(BF16)

    inv_freq = ROPE_THETA ** (-jnp.arange(0, ROPE_DIM, 2, dtype=F32) / ROPE_DIM)
    lane = jnp.arange(LANES)
    in_head = lane % DIFF_HEAD_DIM
    freq = jnp.where(in_head < ROPE_DIM, inv_freq[lane % (ROPE_DIM // 2)], 0.0)[None, :]
    sin_up = ((in_head >= ROPE_DIM // 2) & (in_head < ROPE_DIM)).astype(F32)[None, :]
    sin_dn = -(in_head < ROPE_DIM // 2).astype(F32)[None, :]
    pos = positions.reshape(bsz * seq, 1)

    for layer in range(DEPTH):
        i = layer // 2
        if layer % 2 == 0:
            w_in = ssm_w_in[i]
            s0, s1, s2, s3 = (SSD_INNER, SSD_INNER + SSD_CONV_DIM,
                              SSD_INNER + SSD_CONV_DIM + SSD_HEADS,
                              SSD_INNER + SSD_CONV_DIM + SSD_HEADS + LRU_WIDTH)
            w_perm = jnp.concatenate(
                [w_in[:, :s0], w_in[:, s2:s3], w_in[:, s3:], w_in[:, s0:s1], w_in[:, s1:s2],
                 jnp.zeros((d, LANES - SSD_HEADS), w_in.dtype)], axis=1).astype(BF16)
            ssd_params = (ssm_conv_w[i], row(ssm_conv_b[i]),
                          _pad_lanes(ssm_dt_bias[i]), _pad_lanes(ssm_a_log[i]),
                          row(jnp.repeat(ssm_d[i], SSD_HEAD_DIM)), row(ssm_norm_w[i]), expand)
            lru_params = (lru_conv_w[i], row(lru_conv_b[i]),
                          _block_diag_tiles(lru_w_a[i]).astype(BF16), row(lru_b_a[i].reshape(-1)),
                          _block_diag_tiles(lru_w_x[i]).astype(BF16), row(lru_b_x[i].reshape(-1)),
                          row(lru_lambda[i]))
            h = _mixer_ln(h, bsz, seq, w_perm, ssd_params, lru_params, mix_w_out[i].astype(BF16),
                          row(ln1_g[layer]), row(ln1_b[layer]))
        else:
            lambda_init = 0.8 - 0.6 * math.exp(-0.3 * layer)
            qt, k, vt = _qkv_rope(h, attn_w_qkv[i].astype(BF16), pos, freq, sin_up, sin_dn,
                                  bsz, seq, ATTN_BLOCK)
            o = _attn(qt, k, vt, row(attn_lq1[i]), row(attn_lk1[i]), row(attn_lq2[i]),
                      row(attn_lk2[i]), attn_subln_w[i][:, None], lambda_init)
            h = _outproj_ln([o], [attn_w_out[i].astype(BF16)], h,
                            row(ln1_g[layer]), row(ln1_b[layer]))
        h = _mlp_ln(h, ff_w1[layer].astype(BF16), ff_w2[layer].astype(BF16),
                    row(ln2_g[layer]), row(ln2_b[layer]))
    return h.reshape(bsz, seq, d)
```

```python
import functools
import math

import jax
import jax.numpy as jnp
from jax import lax
from jax.experimental import pallas as pl
from jax.experimental.pallas import tpu as pltpu

F32 = jnp.float32
BF16 = jnp.bfloat16

SUBLANES = 8
LANES = 128
BF16_ROWS = 2 * SUBLANES
LOG2_E = math.log2(math.e)

D_MODEL = 1024
DEPTH = 2
SSD_HEADS = 16
SSD_HEAD_DIM = 64
SSD_INNER = SSD_HEADS * SSD_HEAD_DIM
SSD_GROUPS = 2
SSD_STATE = 128
SSD_CHUNK = 128
SSD_CONV_DIM = SSD_INNER + 2 * SSD_GROUPS * SSD_STATE
CONV_WIDTH = 4
LRU_HEADS = 16
LRU_WIDTH = D_MODEL
LRU_BLOCK = LRU_WIDTH // LRU_HEADS
LRU_C = 8.0
LRU_TILE = 256
DIFF_HEADS = 8
DIFF_HEAD_DIM = 64
DIFF_V_DIM = 2 * DIFF_HEAD_DIM
ROPE_DIM = DIFF_HEAD_DIM // 4
ROPE_THETA = 500000.0
ATTN_BLOCK = 512
ATTN_Q_RATIO = 4
ATTN_Q_PIECE = 256
D_FF = 4 * D_MODEL
MLP_FF_CHUNK = 1024
ALPHA = (2 * DEPTH) ** 0.25
EPS = 1e-5

PROJ_COLS = 3 * D_MODEL + SSD_CONV_DIM + LANES
MIX_ROWS = 256
PROJ_PIECE = 256
PROJ_TICKS_PER_PIECE = 2

VMEM_LIMIT = 56 * 1024 * 1024


def _cparams(*sem):
    return pltpu.CompilerParams(dimension_semantics=sem, vmem_limit_bytes=VMEM_LIMIT)


def _layer_norm(v, g, b):
    mu = jnp.mean(v, axis=-1, keepdims=True)
    d = v - mu
    var = jnp.mean(d * d, axis=-1, keepdims=True)
    return d * lax.rsqrt(var + EPS) * g + b


def _dot(a, b):
    return jnp.dot(a, b, preferred_element_type=F32)


def _dot_nt(a, b):
    return lax.dot_general(a, b, (((1,), (1,)), ((), ())), preferred_element_type=F32)


def _causal_conv(x_ref, tail_ref, cw_ref, cb_ref, rows):
    width = x_ref.shape[-1]
    groups = rows // SUBLANES
    x = x_ref[...].reshape(groups, SUBLANES, width)
    ext = jnp.concatenate([tail_ref[...][None], x], axis=0)
    sub = lax.broadcasted_iota(jnp.int32, (groups, SUBLANES, width), 1)
    out = cb_ref[...] + cw_ref[CONV_WIDTH - 1:CONV_WIDTH, :] * x
    for s in range(1, CONV_WIDTH):
        rot = pltpu.roll(ext, s, 1)
        shifted = jnp.where(sub >= s, rot[1:], rot[:-1])
        out = out + cw_ref[CONV_WIDTH - 1 - s:CONV_WIDTH - s, :] * shifted
    tail_ref[...] = x[groups - 1]
    return out.reshape(rows, width)


def _expand_heads(v, e):
    hi = v.astype(BF16)
    lo = (v - hi.astype(F32)).astype(BF16)
    return _dot(hi, e) + _dot(lo, e)


def _ssd_chunk(z_ref, xbc_ref, dt_ref, cw_ref, cb_ref, dtb_ref, alog_ref, dsk_ref, nw_ref, e_ref,
               xpad_ref, st_ref, tick):
    l = SSD_CHUNK
    gw = SSD_INNER // SSD_GROUPS
    tick()
    xbc = jax.nn.silu(_causal_conv(xbc_ref, xpad_ref, cw_ref, cb_ref, l))
    tick()
    xs = xbc[:, :SSD_INNER]
    bm = xbc[:, SSD_INNER:SSD_INNER + SSD_GROUPS * SSD_STATE]
    cm = xbc[:, SSD_INNER + SSD_GROUPS * SSD_STATE:]

    dt = jax.nn.softplus(dt_ref[...] + dtb_ref[...])
    adt = dt * (-jnp.exp(alog_ref[...]))
    row = lax.broadcasted_iota(jnp.int32, (l, l), 0)
    col = lax.broadcasted_iota(jnp.int32, (l, l), 1)
    causal = row >= col
    a_cum = jnp.dot(causal.astype(F32), adt, precision=lax.Precision.HIGHEST,
                    preferred_element_type=F32)
    a_cum_t = a_cum.T
    a_last = a_cum[l - 1:l, :]
    ea = jnp.exp(a_cum)
    stacked = jnp.concatenate([dt, ea, dt * jnp.exp(a_last - a_cum)], axis=0)
    ex = _expand_heads(stacked, e_ref[...])
    dt_x, ea_x, ds_x = ex[:l], ex[l:2 * l], ex[2 * l:]
    xdt = xs * dt_x
    x2 = (xs * ds_x).astype(BF16)
    cd_x = ea_x[l - 1:l, :]

    lane = lax.broadcasted_iota(jnp.int32, (l, LANES), 1)
    lo_half = lane < SSD_HEAD_DIM
    y_parts = []
    for g in range(SSD_GROUPS):
        bg = bm[:, g * SSD_STATE:(g + 1) * SSD_STATE]
        cg = cm[:, g * SSD_STATE:(g + 1) * SSD_STATE].astype(BF16)
        cb = _dot_nt(cg, bg.astype(BF16))
        heads_per_group = SSD_HEADS // SSD_GROUPS
        diag = []
        for pair in range(heads_per_group // 2):
            tick()
            blk = g * (heads_per_group // 2) + pair
            xp = xdt[:, blk * LANES:(blk + 1) * LANES]
            acc = None
            for half in range(2):
                h = 2 * blk + half
                seg = a_cum[:, h:h + 1] - a_cum_t[h:h + 1, :]
                dec = jnp.exp(jnp.where(causal, seg, -jnp.inf))
                m = (cb * dec).astype(BF16)
                keep = lo_half if half == 0 else jnp.logical_not(lo_half)
                part = _dot(m, jnp.where(keep, xp, 0.0).astype(BF16))
                acc = part if acc is None else acc + part
            diag.append(acc)
        y_diag = jnp.concatenate(diag, axis=1)
        prev = st_ref[g]
        y_off = _dot(cg, prev.astype(BF16)) * ea_x[:, g * gw:(g + 1) * gw]
        st_new = _dot(bg.T.astype(BF16), x2[:, g * gw:(g + 1) * gw])
        st_ref[g] = prev * cd_x[:, g * gw:(g + 1) * gw] + st_new
        y_parts.append(y_diag + y_off)

    tick()
    y = jnp.concatenate(y_parts, axis=1) + dsk_ref[...] * xs
    y = y * jax.nn.silu(z_ref[...])
    normed = []
    for g in range(SSD_GROUPS):
        yg = y[:, g * gw:(g + 1) * gw]
        normed.append(yg * lax.rsqrt(jnp.mean(yg * yg, axis=-1, keepdims=True) + EPS))
    return jnp.concatenate(normed, axis=1) * nw_ref[...]


def _lru_block(gate_ref, xl_ref, cw_ref, cb_ref, wa_ref, ba_ref, wx_ref, bx_ref, lam_ref,
               xpad_ref, h_ref, rows, tick):
    tick()
    xl = _causal_conv(xl_ref, xpad_ref, cw_ref, cb_ref, rows)
    tick()
    xlb = xl.astype(BF16)
    n_tiles = LRU_WIDTH // LRU_TILE

    def gate(w_ref, b_ref):
        parts = [_dot(xlb[:, t * LRU_TILE:(t + 1) * LRU_TILE], w_ref[t]) for t in range(n_tiles)]
        return jax.nn.sigmoid(jnp.concatenate(parts, axis=1) + b_ref[...])

    r = gate(wa_ref, ba_ref)
    tick()
    i = gate(wx_ref, bx_ref)
    tick()
    log_a = -LRU_C * r * jax.nn.softplus(-lam_ref[...])
    a = jnp.exp(log_a)
    mult = jnp.sqrt(jnp.maximum(1.0 - jnp.exp(2.0 * log_a), 0.0))
    tick()
    u = mult * (i * xl)

    groups = rows // SUBLANES
    a = a.reshape(groups, SUBLANES, LRU_WIDTH)
    u = u.reshape(groups, SUBLANES, LRU_WIDTH)
    sub = lax.broadcasted_iota(jnp.int32, (groups, SUBLANES, LRU_WIDTH), 1)
    k = 1
    while k < SUBLANES:
        valid = sub >= k
        a_sh = jnp.where(valid, pltpu.roll(a, k, 1), 1.0)
        u_sh = jnp.where(valid, pltpu.roll(u, k, 1), 0.0)
        u = a * u_sh + u
        a = a * a_sh
        k *= 2
        tick()
    carry = h_ref[...]
    hs = []
    for gi in range(groups):
        if gi % SUBLANES == 0:
            tick()
        hg = u[gi] + a[gi] * carry
        hs.append(hg)
        carry = jnp.broadcast_to(hg[SUBLANES - 1:SUBLANES, :], (SUBLANES, LRU_WIDTH))
    h_ref[...] = carry
    tick()
    h = jnp.concatenate(hs, axis=0)
    return h * jax.nn.gelu(gate_ref[...])


_Z0, _GATE0, _XL0, _XBC0, _DT0 = 0, D_MODEL, 2 * D_MODEL, 3 * D_MODEL, 3 * D_MODEL + SSD_CONV_DIM


def _mixer_kernel(x_ref, xn_ref, win_ref,
                  s_cw, s_cb, s_dtb, s_alog, s_dsk, s_nw, s_e,
                  l_cw, l_cb, l_wa, l_ba, l_wx, l_bx, l_lam,
                  wout_ref, g_ref, b_ref, o_ref,
                  pa_ref, pb_ref, s_xpad, s_st, l_xpad, l_h):
    r = MIX_ROWS
    first = jnp.logical_and(pl.program_id(0) == 0, pl.program_id(1) == 0)

    def projection_pieces(rows_ref, dst_ref):
        xb = rows_ref[...].astype(BF16)

        def piece(c0, width):
            dst_ref[:, c0:c0 + width] = _dot(xb, win_ref[:, c0:c0 + width])

        return [functools.partial(piece, c0, min(PROJ_PIECE, PROJ_COLS - c0))
                for c0 in range(0, PROJ_COLS, PROJ_PIECE)]

    def drain(pieces):
        while pieces:
            pieces.pop(0)()

    @pl.when(first)
    def _():
        drain(projection_pieces(x_ref.at[0:r], pa_ref))

    @pl.when(pl.program_id(1) == 0)
    def _():
        s_xpad[...] = jnp.zeros_like(s_xpad)
        s_st[...] = jnp.zeros_like(s_st)
        l_xpad[...] = jnp.zeros_like(l_xpad)
        l_h[...] = jnp.zeros_like(l_h)

    def mix(p_ref, row0, pieces):
        ticks = [0]

        def tick():
            ticks[0] += 1
            if pieces and ticks[0] % PROJ_TICKS_PER_PIECE == 0:
                pieces.pop(0)()

        ya = []
        for c0 in range(0, r, SSD_CHUNK):
            rows = pl.ds(c0, SSD_CHUNK)
            ya.append(_ssd_chunk(p_ref.at[rows, pl.ds(_Z0, SSD_INNER)],
                                 p_ref.at[rows, pl.ds(_XBC0, SSD_CONV_DIM)],
                                 p_ref.at[rows, pl.ds(_DT0, LANES)],
                                 s_cw, s_cb, s_dtb, s_alog, s_dsk, s_nw, s_e, s_xpad, s_st, tick))
        y_a = jnp.concatenate(ya, axis=0).astype(BF16)
        y_b = _lru_block(p_ref.at[:, pl.ds(_GATE0, LRU_WIDTH)], p_ref.at[:, pl.ds(_XL0, LRU_WIDTH)],
                         l_cw, l_cb, l_wa, l_ba, l_wx, l_bx, l_lam, l_xpad, l_h, r, tick).astype(BF16)
        drain(pieces)
        mixed = _dot(y_a, wout_ref[0:SSD_INNER, :]) + _dot(y_b, wout_ref[SSD_INNER:, :])
        o_ref[row0:row0 + r, :] = _layer_norm(ALPHA * x_ref[row0:row0 + r, :] + mixed,
                                              g_ref[...], b_ref[...])

    mix(pa_ref, 0, projection_pieces(x_ref.at[r:2 * r], pb_ref))
    mix(pb_ref, r, projection_pieces(xn_ref, pa_ref))


def _mixer_ln(x2d, bsz, seq, w_in, ssd_params, lru_params, w_out, g, b):
    r = MIX_ROWS
    t, d = x2d.shape
    steps = seq // (2 * r)
    last_blk = t // r - 1
    const = lambda a: pl.BlockSpec(a.shape, lambda bi, c: (0,) * a.ndim, pipeline_mode=pl.Buffered(1))
    params = [w_in, *ssd_params, *lru_params, w_out, g, b]
    return pl.pallas_call(
        _mixer_kernel,
        grid=(bsz, steps),
        in_specs=[pl.BlockSpec((2 * r, d), lambda bi, c: (bi * steps + c, 0)),
                  pl.BlockSpec((r, d), lambda bi, c: (jnp.minimum(2 * (bi * steps + c) + 2, last_blk), 0)),
                  *[const(a) for a in params]],
        out_specs=pl.BlockSpec((2 * r, d), lambda bi, c: (bi * steps + c, 0)),
        out_shape=jax.ShapeDtypeStruct((t, d), F32),
        scratch_shapes=[pltpu.VMEM((r, PROJ_COLS), F32), pltpu.VMEM((r, PROJ_COLS), F32),
                        pltpu.VMEM((SUBLANES, SSD_CONV_DIM), F32),
                        pltpu.VMEM((SSD_GROUPS, SSD_STATE, SSD_INNER // SSD_GROUPS), F32),
                        pltpu.VMEM((SUBLANES, LRU_WIDTH), F32),
                        pltpu.VMEM((SUBLANES, LRU_WIDTH), F32)],
        compiler_params=_cparams("arbitrary", "arbitrary"),
        name="mixer_ln",
    )(x2d, x2d, *params)


def _outproj_ln_kernel(*refs, n_in):
    a_refs, w_refs = refs[:n_in], refs[n_in:2 * n_in]
    res_ref, g_ref, b_ref, o_ref = refs[2 * n_in:]
    acc = ALPHA * res_ref[...]
    for a_ref, w_ref in zip(a_refs, w_refs):
        acc = acc + _dot(a_ref[...], w_ref[...])
    o_ref[...] = _layer_norm(acc, g_ref[...], b_ref[...])


def _outproj_ln(acts, weights, res, g, b, tm=512):
    t, d = res.shape
    n_in = len(acts)
    row = lambda width: pl.BlockSpec((tm, width), lambda i: (i, 0))
    return pl.pallas_call(
        functools.partial(_outproj_ln_kernel, n_in=n_in),
        grid=(t // tm,),
        in_specs=([row(a.shape[1]) for a in acts]
                  + [pl.BlockSpec(w.shape, lambda i: (0, 0)) for w in weights]
                  + [row(d), pl.BlockSpec((1, d), lambda i: (0, 0)),
                     pl.BlockSpec((1, d), lambda i: (0, 0))]),
        out_specs=row(d),
        out_shape=jax.ShapeDtypeStruct((t, d), F32),
        compiler_params=_cparams("arbitrary"),
        name="outproj_ln",
    )(*acts, *weights, res, g, b)


def _mlp_residual(x, w1_ref, w2_ref, after_first_chunk=None):
    xb = x.astype(BF16)
    acc = ALPHA * x
    for c0 in range(0, D_FF, MLP_FF_CHUNK):
        h = jnp.square(jnp.maximum(_dot(xb, w1_ref[:, c0:c0 + MLP_FF_CHUNK]), 0.0))
        acc = acc + _dot(h.astype(BF16), w2_ref[c0:c0 + MLP_FF_CHUNK, :])
        if c0 == 0 and after_first_chunk is not None:
            after_first_chunk()
    return acc


def _mlp_ln_kernel(x_ref, w1_ref, w2_ref, g_ref, b_ref, o_ref):
    half = x_ref.shape[0] // 2

    def finish(r0, acc):
        o_ref[r0:r0 + half, :] = _layer_norm(acc, g_ref[...], b_ref[...])

    acc_a = _mlp_residual(x_ref[0:half, :], w1_ref, w2_ref)
    acc_b = _mlp_residual(x_ref[half:, :], w1_ref, w2_ref,
                          after_first_chunk=functools.partial(finish, 0, acc_a))
    finish(half, acc_b)


def _mlp_ln(x2d, w1, w2, g, b, tm=1024):
    t, d = x2d.shape
    const = lambda a: pl.BlockSpec(a.shape, lambda i: (0,) * a.ndim, pipeline_mode=pl.Buffered(1))
    return pl.pallas_call(
        _mlp_ln_kernel,
        grid=(t // tm,),
        in_specs=[pl.BlockSpec((tm, d), lambda i: (i, 0)), const(w1), const(w2), const(g), const(b)],
        out_specs=pl.BlockSpec((tm, d), lambda i: (i, 0)),
        out_shape=jax.ShapeDtypeStruct((t, d), F32),
        compiler_params=_cparams("arbitrary"),
        name="mlp_ln",
    )(x2d, w1, w2, g, b)


def _qkv_rope_kernel(x_ref, w_ref, pos_ref, freq_ref, s1_ref, s2_ref, qt_ref, k_ref, vt_ref):
    qk_w = DIFF_HEADS * DIFF_V_DIM
    acc = _dot(x_ref[...].astype(BF16), w_ref[...])
    ang = pos_ref[...].astype(F32) * freq_ref[...]
    cos, sin = jnp.cos(ang), jnp.sin(ang)
    reps = qk_w // LANES
    cos_w = jnp.concatenate([cos] * reps, axis=1)
    sin1_w = jnp.concatenate([sin * s1_ref[...]] * reps, axis=1)
    sin2_w = jnp.concatenate([sin * s2_ref[...]] * reps, axis=1)
    half = ROPE_DIM // 2

    def rope(t):
        return (t * cos_w + pltpu.roll(t, half, 1) * sin1_w
                + pltpu.roll(t, qk_w - half, 1) * sin2_w)

    rows = acc.shape[0]
    q = rope(acc[:, :qk_w]) * (DIFF_HEAD_DIM ** -0.5 * LOG2_E)
    qt_ref[...] = q.T.astype(qt_ref.dtype).reshape(DIFF_HEADS, DIFF_V_DIM, rows)
    k_ref[...] = rope(acc[:, qk_w:2 * qk_w]).astype(k_ref.dtype)
    vt_ref[...] = acc[:, 2 * qk_w:].T.astype(vt_ref.dtype).reshape(DIFF_HEADS, DIFF_V_DIM, rows)


def _qkv_rope(x2d, w, pos, freq, s1, s2, bsz, seq, blk):
    t, d = x2d.shape
    n = w.shape[1]
    nb = seq // blk
    vec = pl.BlockSpec((1, LANES), lambda i: (0, 0))
    tr_spec = pl.BlockSpec((None, DIFF_HEADS, None, DIFF_V_DIM, blk),
                           lambda i: (i // nb, 0, i % nb, 0, 0))
    tr_shape = jax.ShapeDtypeStruct((bsz, DIFF_HEADS, nb, DIFF_V_DIM, blk), BF16)
    return pl.pallas_call(
        _qkv_rope_kernel,
        grid=(t // blk,),
        in_specs=[pl.BlockSpec((blk, d), lambda i: (i, 0)),
                  pl.BlockSpec((d, n), lambda i: (0, 0)),
                  pl.BlockSpec((blk, 1), lambda i: (i, 0)),
                  vec, vec, vec],
        out_specs=[tr_spec, pl.BlockSpec((blk, n // 3), lambda i: (i, 0)), tr_spec],
        out_shape=[tr_shape, jax.ShapeDtypeStruct((t, n // 3), BF16), tr_shape],
        compiler_params=_cparams("arbitrary"),
        name="qkv_rope",
    )(x2d, w, pos, freq, s1, s2)


def _attn_kernel(qt_ref, k_ref, vt_ref, lq1_ref, lk1_ref, lq2_ref, lk2_ref, sw_ref, o_ref,
                 m_ref, acc_ref, sta_ref, stb_ref, *, blk, ratio, lambda_init):
    assert ratio % 2 == 0
    qb = ratio * blk
    qt = jnp.concatenate([qt_ref[r] for r in range(ratio)], axis=1).astype(F32)
    first = lax.broadcasted_iota(jnp.int32, (DIFF_V_DIM, qb), 0) < DIFF_HEAD_DIM
    qts = (jnp.where(first, qt, 0.0).astype(BF16), jnp.where(first, 0.0, qt).astype(BF16))

    m_ref[...] = jnp.full_like(m_ref, -jnp.inf)
    acc_ref[...] = jnp.zeros_like(acc_ref)
    ones_rows = jnp.ones((BF16_ROWS, blk), BF16)

    w = ATTN_Q_PIECE

    def scores(j, st_ref, c0):
        start = pl.multiple_of(j * blk, blk)
        k = k_ref[pl.ds(start, blk), :]
        for comp in range(2):
            st_ref[comp, :, c0:c0 + w] = _dot(k, qts[comp][:, c0:c0 + w])

    def accumulate(j, st_ref, c0, diag_off=None):
        vt = jnp.concatenate([vt_ref[j], ones_rows], axis=0)
        if diag_off is not None:
            key = lax.broadcasted_iota(jnp.int32, (blk, w), 0)
            qry = lax.broadcasted_iota(jnp.int32, (blk, w), 1) + diag_off
            keep = key <= qry
        for comp in range(2):
            st = st_ref[comp, :, c0:c0 + w]
            if diag_off is not None:
                st = jnp.where(keep, st, -jnp.inf)
            m_prev = m_ref[comp, :, c0:c0 + w]
            m_new = jnp.maximum(m_prev, jnp.max(st, axis=0, keepdims=True))
            alpha = jnp.exp2(m_prev - m_new)
            p = jnp.exp2(st - m_new)
            acc_ref[comp, :, c0:c0 + w] = (alpha * acc_ref[comp, :, c0:c0 + w]
                                           + _dot(vt, p.astype(BF16)))
            m_ref[comp, :, c0:c0 + w] = m_new

    n_full = ratio * pl.program_id(2)
    slots = (sta_ref, stb_ref)

    def pair(u, carry):
        for half in range(2):
            j = 2 * u + half
            for c0 in range(0, qb, w):
                scores(j + 1, slots[1 - half], c0)
                accumulate(j, slots[half], c0)
        return carry

    for c0 in range(0, qb, w):
        scores(0, sta_ref, c0)
    lax.fori_loop(0, n_full // 2, pair, 0)
    for s in range(ratio):
        for c0 in range(s * blk, qb, w):
            if s + 1 < ratio and c0 >= (s + 1) * blk:
                scores(n_full + s + 1, slots[(s + 1) % 2], c0)
            accumulate(n_full + s, slots[s % 2], c0,
                       diag_off=c0 - s * blk if c0 < (s + 1) * blk else None)

    lam = (jnp.exp(jnp.sum(lq1_ref[...] * lk1_ref[...], axis=-1, keepdims=True))
           - jnp.exp(jnp.sum(lq2_ref[...] * lk2_ref[...], axis=-1, keepdims=True)) + lambda_init)
    hd = DIFF_V_DIM
    outs = [acc_ref[comp, :hd, :] / acc_ref[comp, hd:hd + 1, :] for comp in range(2)]
    ot = outs[0] - lam * outs[1]
    ot = ot * lax.rsqrt(jnp.mean(ot * ot, axis=0, keepdims=True) + EPS)
    ot = ot * sw_ref[...] * (1.0 - lambda_init)
    o_ref[...] = ot.T.astype(o_ref.dtype)


def _attn(qt, k, vt, lq1, lk1, lq2, lk2, sw_col, lambda_init, ratio=ATTN_Q_RATIO):
    bsz, heads, nb, hd, blk = qt.shape
    seq = nb * blk
    qb = ratio * blk
    nq = nb // ratio
    vec = lambda w: pl.BlockSpec((1, w), lambda b, h, i: (0, 0))
    return pl.pallas_call(
        functools.partial(_attn_kernel, blk=blk, ratio=ratio, lambda_init=lambda_init),
        grid=(bsz, heads, nq),
        in_specs=[pl.BlockSpec((None, None, ratio, hd, blk), lambda b, h, i: (b, h, i, 0, 0)),
                  pl.BlockSpec((seq, hd), lambda b, h, i: (b, h)),
                  pl.BlockSpec((None, None, nb, hd, blk), lambda b, h, i: (b, h, 0, 0, 0)),
                  vec(DIFF_HEAD_DIM), vec(DIFF_HEAD_DIM), vec(DIFF_HEAD_DIM), vec(DIFF_HEAD_DIM),
                  pl.BlockSpec((hd, 1), lambda b, h, i: (0, 0))],
        out_specs=pl.BlockSpec((qb, hd), lambda b, h, i: (b * nq + i, h)),
        out_shape=jax.ShapeDtypeStruct((bsz * seq, heads * hd), BF16),
        scratch_shapes=[pltpu.VMEM((2, 1, qb), F32),
                        pltpu.VMEM((2, hd + BF16_ROWS, qb), F32),
                        pltpu.VMEM((2, blk, qb), F32), pltpu.VMEM((2, blk, qb), F32)],
        compiler_params=_cparams("arbitrary", "arbitrary", "arbitrary"),
        name="diff_attn",
    )(qt, k, vt, lq1, lk1, lq2, lk2, sw_col)


def _block_diag_tiles(w):
    per = LRU_TILE // LRU_BLOCK
    w4 = w.reshape(LRU_HEADS // per, per, LRU_BLOCK, LRU_BLOCK)
    bd = jnp.einsum('tgij,gh->tgihj', w4, jnp.eye(per, dtype=w.dtype))
    return bd.reshape(LRU_HEADS // per, LRU_TILE, LRU_TILE)


def _pad_lanes(v):
    return jnp.pad(v, (0, LANES - v.shape[0]))[None, :]


def kernel(x, positions, ssm_w_in, ssm_conv_w, ssm_conv_b, ssm_dt_bias, ssm_a_log, ssm_d, ssm_norm_w, lru_conv_w, lru_conv_b, lru_w_a, lru_b_a, lru_w_x, lru_b_x, lru_lambda, mix_w_out, attn_w_qkv, attn_lq1, attn_lk1, attn_lq2, attn_lk2, attn_subln_w, attn_w_out, ln1_g, ln1_b, ff_w1, ff_w2, ln2_g, ln2_b):
    bsz, seq, d = x.shape
    h = x.reshape(bsz * seq, d)
    row = lambda v: v[None, :]

    head_of_lane = jnp.arange(SSD_INNER) // SSD_HEAD_DIM
    expand = (jnp.arange(LANES)[:, None] == head_of_lane[None, :]).astype(BF16)

    inv_freq = ROPE_THETA ** (-jnp.arange(0, ROPE_DIM, 2, dtype=F32) / ROPE_DIM)
    lane = jnp.arange(LANES)
    in_head = lane % DIFF_HEAD_DIM
    freq = jnp.where(in_head < ROPE_DIM, inv_freq[lane % (ROPE_DIM // 2)], 0.0)[None, :]
    sin_up = ((in_head >= ROPE_DIM // 2) & (in_head < ROPE_DIM)).astype(F32)[None, :]
    sin_dn = -(in_head < ROPE_DIM // 2).astype(F32)[None, :]
    pos = positions.reshape(bsz * seq, 1)

    for layer in range(DEPTH):
        i = layer // 2
        if layer % 2 == 0:
            w_in = ssm_w_in[i]
            s0, s1, s2, s3 = (SSD_INNER, SSD_INNER + SSD_CONV_DIM,
                              SSD_INNER + SSD_CONV_DIM + SSD_HEADS,
                              SSD_INNER + SSD_CONV_DIM + SSD_HEADS + LRU_WIDTH)
            w_perm = jnp.concatenate(
                [w_in[:, :s0], w_in[:, s2:s3], w_in[:, s3:], w_in[:, s0:s1], w_in[:, s1:s2],
                 jnp.zeros((d, LANES - SSD_HEADS), w_in.dtype)], axis=1).astype(BF16)
            ssd_params = (ssm_conv_w[i], row(ssm_conv_b[i]),
                          _pad_lanes(ssm_dt_bias[i]), _pad_lanes(ssm_a_log[i]),
                          row(jnp.repeat(ssm_d[i], SSD_HEAD_DIM)), row(ssm_norm_w[i]), expand)
            lru_params = (lru_conv_w[i], row(lru_conv_b[i]),
                          _block_diag_tiles(lru_w_a[i]).astype(BF16), row(lru_b_a[i].reshape(-1)),
                          _block_diag_tiles(lru_w_x[i]).astype(BF16), row(lru_b_x[i].reshape(-1)),
                          row(lru_lambda[i]))
            h = _mixer_ln(h, bsz, seq, w_perm, ssd_params, lru_params, mix_w_out[i].astype(BF16),
                          row(ln1_g[layer]), row(ln1_b[layer]))
        else:
            lambda_init = 0.8 - 0.6 * math.exp(-0.3 * layer)
            qt, k, vt = _qkv_rope(h, attn_w_qkv[i].astype(BF16), pos, freq, sin_up, sin_dn,
                                  bsz, seq, ATTN_BLOCK)
            o = _attn(qt, k, vt, row(attn_lq1[i]), row(attn_lk1[i]), row(attn_lq2[i]),
                      row(attn_lk2[i]), attn_subln_w[i][:, None], lambda_init)
            h = _outproj_ln([o], [attn_w_out[i].astype(BF16)], h,
                            row(ln1_g[layer]), row(ln1_b[layer]))
        h = _mlp_ln(h, ff_w1[layer].astype(BF16), ff_w2[layer].astype(BF16),
                    row(ln2_g[layer]), row(ln2_b[layer]))
    return h.reshape(bsz, seq, d)
```

```python
import functools
import math

import jax
import jax.numpy as jnp
from jax import lax
from jax.experimental import pallas as pl
from jax.experimental.pallas import tpu as pltpu

F32 = jnp.float32
BF16 = jnp.bfloat16

SUBLANES = 8
LANES = 128
BF16_ROWS = 2 * SUBLANES
LOG2_E = math.log2(math.e)

D_MODEL = 1024
DEPTH = 2
SSD_HEADS = 16
SSD_HEAD_DIM = 64
SSD_INNER = SSD_HEADS * SSD_HEAD_DIM
SSD_GROUPS = 2
SSD_STATE = 128
SSD_CHUNK = 128
SSD_CONV_DIM = SSD_INNER + 2 * SSD_GROUPS * SSD_STATE
CONV_WIDTH = 4
LRU_HEADS = 16
LRU_WIDTH = D_MODEL
LRU_BLOCK = LRU_WIDTH // LRU_HEADS
LRU_C = 8.0
LRU_TILE = 256
DIFF_HEADS = 8
DIFF_HEAD_DIM = 64
DIFF_V_DIM = 2 * DIFF_HEAD_DIM
ROPE_DIM = DIFF_HEAD_DIM // 4
ROPE_THETA = 500000.0
ATTN_BLOCK = 512
ATTN_Q_RATIO = 4
ATTN_Q_PIECE = 256
D_FF = 4 * D_MODEL
MLP_FF_CHUNK = 1024
ALPHA = (2 * DEPTH) ** 0.25
EPS = 1e-5

PROJ_COLS = 3 * D_MODEL + SSD_CONV_DIM + LANES
MIX_ROWS = 256
PROJ_PIECE = 256
PROJ_TICKS_PER_PIECE = 2

VMEM_LIMIT = 56 * 1024 * 1024


def _cparams(*sem):
    return pltpu.CompilerParams(dimension_semantics=sem, vmem_limit_bytes=VMEM_LIMIT)


def _layer_norm(v, g, b):
    mu = jnp.mean(v, axis=-1, keepdims=True)
    d = v - mu
    var = jnp.mean(d * d, axis=-1, keepdims=True)
    return d * lax.rsqrt(var + EPS) * g + b


def _dot(a, b):
    return jnp.dot(a, b, preferred_element_type=F32)


def _dot_nt(a, b):
    return lax.dot_general(a, b, (((1,), (1,)), ((), ())), preferred_element_type=F32)


def _causal_conv(x_ref, tail_ref, cw_ref, cb_ref, rows):
    width = x_ref.shape[-1]
    groups = rows // SUBLANES
    x = x_ref[...].reshape(groups, SUBLANES, width)
    ext = jnp.concatenate([tail_ref[...][None], x], axis=0)
    sub = lax.broadcasted_iota(jnp.int32, (groups, SUBLANES, width), 1)
    out = cb_ref[...] + cw_ref[CONV_WIDTH - 1:CONV_WIDTH, :] * x
    for s in range(1, CONV_WIDTH):
        rot = pltpu.roll(ext, s, 1)
        shifted = jnp.where(sub >= s, rot[1:], rot[:-1])
        out = out + cw_ref[CONV_WIDTH - 1 - s:CONV_WIDTH - s, :] * shifted
    tail_ref[...] = x[groups - 1]
    return out.reshape(rows, width)


def _expand_heads(v, e):
    hi = v.astype(BF16)
    lo = (v - hi.astype(F32)).astype(BF16)
    return _dot(hi, e) + _dot(lo, e)


def _ssd_chunk(z_ref, xbc_ref, dt_ref, cw_ref, cb_ref, dtb_ref, alog_ref, dsk_ref, nw_ref, e_ref,
               xpad_ref, st_ref, tick):
    l = SSD_CHUNK
    gw = SSD_INNER // SSD_GROUPS
    tick()
    xbc = jax.nn.silu(_causal_conv(xbc_ref, xpad_ref, cw_ref, cb_ref, l))
    tick()
    xs = xbc[:, :SSD_INNER]
    bm = xbc[:, SSD_INNER:SSD_INNER + SSD_GROUPS * SSD_STATE]
    cm = xbc[:, SSD_INNER + SSD_GROUPS * SSD_STATE:]

    dt = jax.nn.softplus(dt_ref[...] + dtb_ref[...])
    adt = dt * (-jnp.exp(alog_ref[...]))
    row = lax.broadcasted_iota(jnp.int32, (l, l), 0)
    col = lax.broadcasted_iota(jnp.int32, (l, l), 1)
    causal = row >= col
    a_cum = jnp.dot(causal.astype(F32), adt, precision=lax.Precision.HIGHEST,
                    preferred_element_type=F32)
    a_cum_t = a_cum.T
    a_last = a_cum[l - 1:l, :]
    ea = jnp.exp(a_cum)
    stacked = jnp.concatenate([dt, ea, dt * jnp.exp(a_last - a_cum)], axis=0)
    ex = _expand_heads(stacked, e_ref[...])
    dt_x, ea_x, ds_x = ex[:l], ex[l:2 * l], ex[2 * l:]
    xdt = xs * dt_x
    x2 = (xs * ds_x).astype(BF16)
    cd_x = ea_x[l - 1:l, :]

    lane = lax.broadcasted_iota(jnp.int32, (l, LANES), 1)
    lo_half = lane < SSD_HEAD_DIM
    y_parts = []
    for g in range(SSD_GROUPS):
        bg = bm[:, g * SSD_STATE:(g + 1) * SSD_STATE]
        cg = cm[:, g * SSD_STATE:(g + 1) * SSD_STATE].astype(BF16)
        cb = _dot_nt(cg, bg.astype(BF16))
        heads_per_group = SSD_HEADS // SSD_GROUPS
        diag = []
        for pair in range(heads_per_group // 2):
            tick()
            blk = g * (heads_per_group // 2) + pair
            xp = xdt[:, blk * LANES:(blk + 1) * LANES]
            acc = None
            for half in range(2):
                h = 2 * blk + half
                seg = a_cum[:, h:h + 1] - a_cum_t[h:h + 1, :]
                dec = jnp.exp(jnp.where(causal, seg, -jnp.inf))
                m = (cb * dec).astype(BF16)
                keep = lo_half if half == 0 else jnp.logical_not(lo_half)
                part = _dot(m, jnp.where(keep, xp, 0.0).astype(BF16))
                acc = part if acc is None else acc + part
            diag.append(acc)
        y_diag = jnp.concatenate(diag, axis=1)
        prev = st_ref[g]
        y_off = _dot(cg, prev.astype(BF16)) * ea_x[:, g * gw:(g + 1) * gw]
        st_new = _dot(bg.T.astype(BF16), x2[:, g * gw:(g + 1) * gw])
        st_ref[g] = prev * cd_x[:, g * gw:(g + 1) * gw] + st_new
        y_parts.append(y_diag + y_off)

    tick()
    y = jnp.concatenate(y_parts, axis=1) + dsk_ref[...] * xs
    y = y * jax.nn.silu(z_ref[...])
    normed = []
    for g in range(SSD_GROUPS):
        yg = y[:, g * gw:(g + 1) * gw]
        normed.append(yg * lax.rsqrt(jnp.mean(yg * yg, axis=-1, keepdims=True) + EPS))
    return jnp.concatenate(normed, axis=1) * nw_ref[...]


def _lru_block(gate_ref, xl_ref, cw_ref, cb_ref, wa_ref, ba_ref, wx_ref, bx_ref, lam_ref,
               xpad_ref, h_ref, rows, tick):
    tick()
    xl = _causal_conv(xl_ref, xpad_ref, cw_ref, cb_ref, rows)
    tick()
    xlb = xl.astype(BF16)
    n_tiles = LRU_WIDTH // LRU_TILE

    def gate(w_ref, b_ref):
        parts = [_dot(xlb[:, t * LRU_TILE:(t + 1) * LRU_TILE], w_ref[t]) for t in range(n_tiles)]
        return jax.nn.sigmoid(jnp.concatenate(parts, axis=1) + b_ref[...])

    r = gate(wa_ref, ba_ref)
    tick()
    i = gate(wx_ref, bx_ref)
    tick()
    log_a = -LRU_C * r * jax.nn.softplus(-lam_ref[...])
    a = jnp.exp(log_a)
    mult = jnp.sqrt(jnp.maximum(1.0 - jnp.exp(2.0 * log_a), 0.0))
    tick()
    u = mult * (i * xl)

    groups = rows // SUBLANES
    a = a.reshape(groups, SUBLANES, LRU_WIDTH)
    u = u.reshape(groups, SUBLANES, LRU_WIDTH)
    sub = lax.broadcasted_iota(jnp.int32, (groups, SUBLANES, LRU_WIDTH), 1)
    k = 1
    while k < SUBLANES:
        valid = sub >= k
        a_sh = jnp.where(valid, pltpu.roll(a, k, 1), 1.0)
        u_sh = jnp.where(valid, pltpu.roll(u, k, 1), 0.0)
        u = a * u_sh + u
        a = a * a_sh
        k *= 2
        tick()
    carry = h_ref[...]
    hs = []
    for gi in range(groups):
        if gi % SUBLANES == 0:
            tick()
        hg = u[gi] + a[gi] * carry
        hs.append(hg)
        carry = jnp.broadcast_to(hg[SUBLANES - 1:SUBLANES, :], (SUBLANES, LRU_WIDTH))
    h_ref[...] = carry
    tick()
    h = jnp.concatenate(hs, axis=0)
    return h * jax.nn.gelu(gate_ref[...])


_Z0, _GATE0, _XL0, _XBC0, _DT0 = 0, D_MODEL, 2 * D_MODEL, 3 * D_MODEL, 3 * D_MODEL + SSD_CONV_DIM


def _mixer_kernel(x_ref, xn_ref, win_ref,
                  s_cw, s_cb, s_dtb, s_alog, s_dsk, s_nw, s_e,
                  l_cw, l_cb, l_wa, l_ba, l_wx, l_bx, l_lam,
                  wout_ref, g_ref, b_ref, o_ref,
                  pa_ref, pb_ref, s_xpad, s_st, l_xpad, l_h):
    r = MIX_ROWS
    first = jnp.logical_and(pl.program_id(0) == 0, pl.program_id(1) == 0)

    def projection_pieces(rows_ref, dst_ref):
        xb = rows_ref[...].astype(BF16)

        def piece(c0, width):
            dst_ref[:, c0:c0 + width] = _dot(xb, win_ref[:, c0:c0 + width])

        return [functools.partial(piece, c0, min(PROJ_PIECE, PROJ_COLS - c0))
                for c0 in range(0, PROJ_COLS, PROJ_PIECE)]

    def drain(pieces):
        while pieces:
            pieces.pop(0)()

    @pl.when(first)
    def _():
        drain(projection_pieces(x_ref.at[0:r], pa_ref))

    @pl.when(pl.program_id(1) == 0)
    def _():
        s_xpad[...] = jnp.zeros_like(s_xpad)
        s_st[...] = jnp.zeros_like(s_st)
        l_xpad[...] = jnp.zeros_like(l_xpad)
        l_h[...] = jnp.zeros_like(l_h)

    def mix(p_ref, row0, pieces):
        ticks = [0]

        def tick():
            ticks[0] += 1
            if pieces and ticks[0] % PROJ_TICKS_PER_PIECE == 0:
                pieces.pop(0)()

        ya = []
        for c0 in range(0, r, SSD_CHUNK):
            rows = pl.ds(c0, SSD_CHUNK)
            ya.append(_ssd_chunk(p_ref.at[rows, pl.ds(_Z0, SSD_INNER)],
                                 p_ref.at[rows, pl.ds(_XBC0, SSD_CONV_DIM)],
                                 p_ref.at[rows, pl.ds(_DT0, LANES)],
                                 s_cw, s_cb, s_dtb, s_alog, s_dsk, s_nw, s_e, s_xpad, s_st, tick))
        y_a = jnp.concatenate(ya, axis=0).astype(BF16)
        y_b = _lru_block(p_ref.at[:, pl.ds(_GATE0, LRU_WIDTH)], p_ref.at[:, pl.ds(_XL0, LRU_WIDTH)],
                         l_cw, l_cb, l_wa, l_ba, l_wx, l_bx, l_lam, l_xpad, l_h, r, tick).astype(BF16)
        drain(pieces)
        mixed = _dot(y_a, wout_ref[0:SSD_INNER, :]) + _dot(y_b, wout_ref[SSD_INNER:, :])
        o_ref[row0:row0 + r, :] = _layer_norm(ALPHA * x_ref[row0:row0 + r, :] + mixed,
                                              g_ref[...], b_ref[...])

    mix(pa_ref, 0, projection_pieces(x_ref.at[r:2 * r], pb_ref))
    mix(pb_ref, r, projection_pieces(xn_ref, pa_ref))


def _mixer_ln(x2d, bsz, seq, w_in, ssd_params, lru_params, w_out, g, b):
    r = MIX_ROWS
    t, d = x2d.shape
    steps = seq // (2 * r)
    last_blk = t // r - 1
    const = lambda a: pl.BlockSpec(a.shape, lambda bi, c: (0,) * a.ndim, pipeline_mode=pl.Buffered(1))
    params = [w_in, *ssd_params, *lru_params, w_out, g, b]
    return pl.pallas_call(
        _mixer_kernel,
        grid=(bsz, steps),
        in_specs=[pl.BlockSpec((2 * r, d), lambda bi, c: (bi * steps + c, 0)),
                  pl.BlockSpec((r, d), lambda bi, c: (jnp.minimum(2 * (bi * steps + c) + 2, last_blk), 0)),
                  *[const(a) for a in params]],
        out_specs=pl.BlockSpec((2 * r, d), lambda bi, c: (bi * steps + c, 0)),
        out_shape=jax.ShapeDtypeStruct((t, d), F32),
        scratch_shapes=[pltpu.VMEM((r, PROJ_COLS), F32), pltpu.VMEM((r, PROJ_COLS), F32),
                        pltpu.VMEM((SUBLANES, SSD_CONV_DIM), F32),
                        pltpu.VMEM((SSD_GROUPS, SSD_STATE, SSD_INNER // SSD_GROUPS), F32),
                        pltpu.VMEM((SUBLANES, LRU_WIDTH), F32),
                        pltpu.VMEM((SUBLANES, LRU_WIDTH), F32)],
        compiler_params=_cparams("arbitrary", "arbitrary"),
        name="mixer_ln",
    )(x2d, x2d, *params)


def _mlp_residual(x, w1_ref, w2_ref, after_first_chunk=None):
    xb = x.astype(BF16)
    acc = ALPHA * x
    for c0 in range(0, D_FF, MLP_FF_CHUNK):
        h = jnp.square(jnp.maximum(_dot(xb, w1_ref[:, c0:c0 + MLP_FF_CHUNK]), 0.0))
        acc = acc + _dot(h.astype(BF16), w2_ref[c0:c0 + MLP_FF_CHUNK, :])
        if c0 == 0 and after_first_chunk is not None:
            after_first_chunk()
    return acc


def _mlp_ln_kernel(x_ref, w1_ref, w2_ref, g_ref, b_ref, o_ref):
    half = x_ref.shape[0] // 2

    def finish(r0, acc):
        o_ref[r0:r0 + half, :] = _layer_norm(acc, g_ref[...], b_ref[...])

    acc_a = _mlp_residual(x_ref[0:half, :], w1_ref, w2_ref)
    acc_b = _mlp_residual(x_ref[half:, :], w1_ref, w2_ref,
                          after_first_chunk=functools.partial(finish, 0, acc_a))
    finish(half, acc_b)


def _mlp_ln(x2d, w1, w2, g, b, tm=1024):
    t, d = x2d.shape
    const = lambda a: pl.BlockSpec(a.shape, lambda i: (0,) * a.ndim, pipeline_mode=pl.Buffered(1))
    return pl.pallas_call(
        _mlp_ln_kernel,
        grid=(t // tm,),
        in_specs=[pl.BlockSpec((tm, d), lambda i: (i, 0)), const(w1), const(w2), const(g), const(b)],
        out_specs=pl.BlockSpec((tm, d), lambda i: (i, 0)),
        out_shape=jax.ShapeDtypeStruct((t, d), F32),
        compiler_params=_cparams("arbitrary"),
        name="mlp_ln",
    )(x2d, w1, w2, g, b)


def _attn_out_mlp_kernel(a_ref, res_ref, wo_ref, g1_ref, b1_ref, w1_ref, w2_ref, g2_ref, b2_ref, o_ref):
    half = a_ref.shape[0] // 2

    def project(r0):
        mixed = _dot(a_ref[r0:r0 + half, :], wo_ref[...])
        return _layer_norm(ALPHA * res_ref[r0:r0 + half, :] + mixed, g1_ref[...], b1_ref[...])

    def finish(r0, acc):
        o_ref[r0:r0 + half, :] = _layer_norm(acc, g2_ref[...], b2_ref[...])

    x1 = [project(0)]
    acc_a = _mlp_residual(x1[0], w1_ref, w2_ref, after_first_chunk=lambda: x1.append(project(half)))
    acc_b = _mlp_residual(x1[1], w1_ref, w2_ref,
                          after_first_chunk=functools.partial(finish, 0, acc_a))
    finish(half, acc_b)


def _attn_out_mlp(a, res, wo, g1, b1, w1, w2, g2, b2, tm=1024):
    t, d = res.shape
    const = lambda p: pl.BlockSpec(p.shape, lambda i: (0,) * p.ndim, pipeline_mode=pl.Buffered(1))
    rows = lambda width: pl.BlockSpec((tm, width), lambda i: (i, 0))
    return pl.pallas_call(
        _attn_out_mlp_kernel,
        grid=(t // tm,),
        in_specs=[rows(a.shape[1]), rows(d), *[const(p) for p in (wo, g1, b1, w1, w2, g2, b2)]],
        out_specs=rows(d),
        out_shape=jax.ShapeDtypeStruct((t, d), F32),
        compiler_params=_cparams("arbitrary"),
        name="attn_out_mlp",
    )(a, res, wo, g1, b1, w1, w2, g2, b2)


def _qkv_rope_kernel(x_ref, w_ref, pos_ref, freq_ref, s1_ref, s2_ref, qt_ref, k_ref, vt_ref):
    qk_w = DIFF_HEADS * DIFF_V_DIM
    acc = _dot(x_ref[...].astype(BF16), w_ref[...])
    ang = pos_ref[...].astype(F32) * freq_ref[...]
    cos, sin = jnp.cos(ang), jnp.sin(ang)
    reps = qk_w // LANES
    cos_w = jnp.concatenate([cos] * reps, axis=1)
    sin1_w = jnp.concatenate([sin * s1_ref[...]] * reps, axis=1)
    sin2_w = jnp.concatenate([sin * s2_ref[...]] * reps, axis=1)
    half = ROPE_DIM // 2

    def rope(t):
        return (t * cos_w + pltpu.roll(t, half, 1) * sin1_w
                + pltpu.roll(t, qk_w - half, 1) * sin2_w)

    rows = acc.shape[0]
    q = rope(acc[:, :qk_w]) * (DIFF_HEAD_DIM ** -0.5 * LOG2_E)
    qt_ref[...] = q.T.astype(qt_ref.dtype).reshape(DIFF_HEADS, DIFF_V_DIM, rows)
    k_ref[...] = rope(acc[:, qk_w:2 * qk_w]).astype(k_ref.dtype)
    vt_ref[...] = acc[:, 2 * qk_w:].T.astype(vt_ref.dtype).reshape(DIFF_HEADS, DIFF_V_DIM, rows)


def _qkv_rope(x2d, w, pos, freq, s1, s2, bsz, seq, blk):
    t, d = x2d.shape
    n = w.shape[1]
    nb = seq // blk
    vec = pl.BlockSpec((1, LANES), lambda i: (0, 0))
    tr_spec = pl.BlockSpec((None, DIFF_HEADS, None, DIFF_V_DIM, blk),
                           lambda i: (i // nb, 0, i % nb, 0, 0))
    tr_shape = jax.ShapeDtypeStruct((bsz, DIFF_HEADS, nb, DIFF_V_DIM, blk), BF16)
    return pl.pallas_call(
        _qkv_rope_kernel,
        grid=(t // blk,),
        in_specs=[pl.BlockSpec((blk, d), lambda i: (i, 0)),
                  pl.BlockSpec((d, n), lambda i: (0, 0)),
                  pl.BlockSpec((blk, 1), lambda i: (i, 0)),
                  vec, vec, vec],
        out_specs=[tr_spec, pl.BlockSpec((blk, n // 3), lambda i: (i, 0)), tr_spec],
        out_shape=[tr_shape, jax.ShapeDtypeStruct((t, n // 3), BF16), tr_shape],
        compiler_params=_cparams("arbitrary"),
        name="qkv_rope",
    )(x2d, w, pos, freq, s1, s2)


def _attn_kernel(qt_ref, k_ref, vt_ref, lq1_ref, lk1_ref, lq2_ref, lk2_ref, sw_ref, o_ref,
                 m_ref, acc_ref, sta_ref, stb_ref, *, blk, ratio, lambda_init):
    assert ratio % 2 == 0
    qb = ratio * blk
    qt = jnp.concatenate([qt_ref[r] for r in range(ratio)], axis=1).astype(F32)
    first = lax.broadcasted_iota(jnp.int32, (DIFF_V_DIM, qb), 0) < DIFF_HEAD_DIM
    qts = (jnp.where(first, qt, 0.0).astype(BF16), jnp.where(first, 0.0, qt).astype(BF16))

    m_ref[...] = jnp.full_like(m_ref, -jnp.inf)
    acc_ref[...] = jnp.zeros_like(acc_ref)
    ones_rows = jnp.ones((BF16_ROWS, blk), BF16)

    w = ATTN_Q_PIECE

    def scores(j, st_ref, c0):
        start = pl.multiple_of(j * blk, blk)
        k = k_ref[pl.ds(start, blk), :]
        for comp in range(2):
            st_ref[comp, :, c0:c0 + w] = _dot(k, qts[comp][:, c0:c0 + w])

    def accumulate(j, st_ref, c0, diag_off=None):
        vt = jnp.concatenate([vt_ref[j], ones_rows], axis=0)
        if diag_off is not None:
            key = lax.broadcasted_iota(jnp.int32, (blk, w), 0)
            qry = lax.broadcasted_iota(jnp.int32, (blk, w), 1) + diag_off
            keep = key <= qry
        for comp in range(2):
            st = st_ref[comp, :, c0:c0 + w]
            if diag_off is not None:
                st = jnp.where(keep, st, -jnp.inf)
            m_prev = m_ref[comp, :, c0:c0 + w]
            m_new = jnp.maximum(m_prev, jnp.max(st, axis=0, keepdims=True))
            alpha = jnp.exp2(m_prev - m_new)
            p = jnp.exp2(st - m_new)
            acc_ref[comp, :, c0:c0 + w] = (alpha * acc_ref[comp, :, c0:c0 + w]
                                           + _dot(vt, p.astype(BF16)))
            m_ref[comp, :, c0:c0 + w] = m_new

    n_full = ratio * pl.program_id(2)
    slots = (sta_ref, stb_ref)

    def pair(u, carry):
        for half in range(2):
            j = 2 * u + half
            for c0 in range(0, qb, w):
                scores(j + 1, slots[1 - half], c0)
                accumulate(j, slots[half], c0)
        return carry

    for c0 in range(0, qb, w):
        scores(0, sta_ref, c0)
    lax.fori_loop(0, n_full // 2, pair, 0)
    for s in range(ratio):
        for c0 in range(s * blk, qb, w):
            if s + 1 < ratio and c0 >= (s + 1) * blk:
                scores(n_full + s + 1, slots[(s + 1) % 2], c0)
            accumulate(n_full + s, slots[s % 2], c0,
                       diag_off=c0 - s * blk if c0 < (s + 1) * blk else None)

    lam = (jnp.exp(jnp.sum(lq1_ref[...] * lk1_ref[...], axis=-1, keepdims=True))
           - jnp.exp(jnp.sum(lq2_ref[...] * lk2_ref[...], axis=-1, keepdims=True)) + lambda_init)
    hd = DIFF_V_DIM
    outs = [acc_ref[comp, :hd, :] / acc_ref[comp, hd:hd + 1, :] for comp in range(2)]
    ot = outs[0] - lam * outs[1]
    ot = ot * lax.rsqrt(jnp.mean(ot * ot, axis=0, keepdims=True) + EPS)
    ot = ot * sw_ref[...] * (1.0 - lambda_init)
    o_ref[...] = ot.T.astype(o_ref.dtype)


def _attn(qt, k, vt, lq1, lk1, lq2, lk2, sw_col, lambda_init, ratio=ATTN_Q_RATIO):
    bsz, heads, nb, hd, blk = qt.shape
    seq = nb * blk
    qb = ratio * blk
    nq = nb // ratio
    vec = lambda w: pl.BlockSpec((1, w), lambda b, h, i: (0, 0))
    return pl.pallas_call(
        functools.partial(_attn_kernel, blk=blk, ratio=ratio, lambda_init=lambda_init),
        grid=(bsz, heads, nq),
        in_specs=[pl.BlockSpec((None, None, ratio, hd, blk), lambda b, h, i: (b, h, i, 0, 0)),
                  pl.BlockSpec((seq, hd), lambda b, h, i: (b, h)),
                  pl.BlockSpec((None, None, nb, hd, blk), lambda b, h, i: (b, h, 0, 0, 0)),
                  vec(DIFF_HEAD_DIM), vec(DIFF_HEAD_DIM), vec(DIFF_HEAD_DIM), vec(DIFF_HEAD_DIM),
                  pl.BlockSpec((hd, 1), lambda b, h, i: (0, 0))],
        out_specs=pl.BlockSpec((qb, hd), lambda b, h, i: (b * nq + i, h)),
        out_shape=jax.ShapeDtypeStruct((bsz * seq, heads * hd), BF16),
        scratch_shapes=[pltpu.VMEM((2, 1, qb), F32),
                        pltpu.VMEM((2, hd + BF16_ROWS, qb), F32),
                        pltpu.VMEM((2, blk, qb), F32), pltpu.VMEM((2, blk, qb), F32)],
        compiler_params=_cparams("arbitrary", "arbitrary", "arbitrary"),
        name="diff_attn",
    )(qt, k, vt, lq1, lk1, lq2, lk2, sw_col)


def _block_diag_tiles(w):
    per = LRU_TILE // LRU_BLOCK
    w4 = w.reshape(LRU_HEADS // per, per, LRU_BLOCK, LRU_BLOCK)
    bd = jnp.einsum('tgij,gh->tgihj', w4, jnp.eye(per, dtype=w.dtype))
    return bd.reshape(LRU_HEADS // per, LRU_TILE, LRU_TILE)


def _pad_lanes(v):
    return jnp.pad(v, (0, LANES - v.shape[0]))[None, :]


def kernel(x, positions, ssm_w_in, ssm_conv_w, ssm_conv_b, ssm_dt_bias, ssm_a_log, ssm_d, ssm_norm_w, lru_conv_w, lru_conv_b, lru_w_a, lru_b_a, lru_w_x, lru_b_x, lru_lambda, mix_w_out, attn_w_qkv, attn_lq1, attn_lk1, attn_lq2, attn_lk2, attn_subln_w, attn_w_out, ln1_g, ln1_b, ff_w1, ff_w2, ln2_g, ln2_b):
    bsz, seq, d = x.shape
    h = x.reshape(bsz * seq, d)
    row = lambda v: v[None, :]

    head_of_lane = jnp.arange(SSD_INNER) // SSD_HEAD_DIM
    expand = (jnp.arange(LANES)[:, None] == head_of_lane[None, :]).astype(BF16)

    inv_freq = ROPE_THETA ** (-jnp.arange(0, ROPE_DIM, 2, dtype=F32) / ROPE_DIM)
    lane = jnp.arange(LANES)
    in_head = lane % DIFF_HEAD_DIM
    freq = jnp.where(in_head < ROPE_DIM, inv_freq[lane % (ROPE_DIM // 2)], 0.0)[None, :]
    sin_up = ((in_head >= ROPE_DIM // 2) & (in_head < ROPE_DIM)).astype(F32)[None, :]
    sin_dn = -(in_head < ROPE_DIM // 2).astype(F32)[None, :]
    pos = positions.reshape(bsz * seq, 1)

    for layer in range(DEPTH):
        i = layer // 2
        mlp_params = (ff_w1[layer].astype(BF16), ff_w2[layer].astype(BF16),
                      row(ln2_g[layer]), row(ln2_b[layer]))
        if layer % 2 == 0:
            w_in = ssm_w_in[i]
            s0, s1, s2, s3 = (SSD_INNER, SSD_INNER + SSD_CONV_DIM,
                              SSD_INNER + SSD_CONV_DIM + SSD_HEADS,
                              SSD_INNER + SSD_CONV_DIM + SSD_HEADS + LRU_WIDTH)
            w_bf = w_in.astype(BF16)
            w_perm = jnp.concatenate(
                [w_bf[:, :s0], w_bf[:, s2:s3], w_bf[:, s3:], w_bf[:, s0:s1], w_bf[:, s1:s2],
                 jnp.zeros((d, LANES - SSD_HEADS), BF16)], axis=1)
            ssd_params = (ssm_conv_w[i], row(ssm_conv_b[i]),
                          _pad_lanes(ssm_dt_bias[i]), _pad_lanes(ssm_a_log[i]),
                          row(jnp.repeat(ssm_d[i], SSD_HEAD_DIM)), row(ssm_norm_w[i]), expand)
            lru_params = (lru_conv_w[i], row(lru_conv_b[i]),
                          _block_diag_tiles(lru_w_a[i]).astype(BF16), row(lru_b_a[i].reshape(-1)),
                          _block_diag_tiles(lru_w_x[i]).astype(BF16), row(lru_b_x[i].reshape(-1)),
                          row(lru_lambda[i]))
            h = _mixer_ln(h, bsz, seq, w_perm, ssd_params, lru_params, mix_w_out[i].astype(BF16),
                          row(ln1_g[layer]), row(ln1_b[layer]))
            h = _mlp_ln(h, *mlp_params)
        else:
            lambda_init = 0.8 - 0.6 * math.exp(-0.3 * layer)
            qt, k, vt = _qkv_rope(h, attn_w_qkv[i].astype(BF16), pos, freq, sin_up, sin_dn,
                                  bsz, seq, ATTN_BLOCK)
            o = _attn(qt, k, vt, row(attn_lq1[i]), row(attn_lk1[i]), row(attn_lq2[i]),
                      row(attn_lk2[i]), attn_subln_w[i][:, None], lambda_init)
            h = _attn_out_mlp(o, h, attn_w_out[i].astype(BF16), row(ln1_g[layer]), row(ln1_b[layer]),
                              *mlp_params)
    return h.reshape(bsz, seq, d)
```

```python
import functools
import math

import jax
import jax.numpy as jnp
from jax import lax
from jax.experimental import pallas as pl
from jax.experimental.pallas import tpu as pltpu

F32 = jnp.float32
BF16 = jnp.bfloat16

SUBLANES = 8
LANES = 128
BF16_ROWS = 2 * SUBLANES
LOG2_E = math.log2(math.e)

D_MODEL = 1024
DEPTH = 2
SSD_HEADS = 16
SSD_HEAD_DIM = 64
SSD_INNER = SSD_HEADS * SSD_HEAD_DIM
SSD_GROUPS = 2
SSD_STATE = 128
SSD_CHUNK = 128
SSD_CONV_DIM = SSD_INNER + 2 * SSD_GROUPS * SSD_STATE
CONV_WIDTH = 4
LRU_HEADS = 16
LRU_WIDTH = D_MODEL
LRU_BLOCK = LRU_WIDTH // LRU_HEADS
LRU_C = 8.0
LRU_TILE = 256
DIFF_HEADS = 8
DIFF_HEAD_DIM = 64
DIFF_V_DIM = 2 * DIFF_HEAD_DIM
ROPE_DIM = DIFF_HEAD_DIM // 4
ROPE_THETA = 500000.0
ATTN_BLOCK = 512
ATTN_Q_RATIO = 4
ATTN_Q_PIECE = 256
D_FF = 4 * D_MODEL
MLP_FF_CHUNK = 1024
ALPHA = (2 * DEPTH) ** 0.25
EPS = 1e-5

PROJ_COLS = 3 * D_MODEL + SSD_CONV_DIM + LANES
MIX_ROWS = 256
PROJ_PIECE = 256
PROJ_TICKS_PER_PIECE = 2

VMEM_LIMIT = 56 * 1024 * 1024


def _cparams(*sem):
    return pltpu.CompilerParams(dimension_semantics=sem, vmem_limit_bytes=VMEM_LIMIT)


def _layer_norm(v, g, b):
    mu = jnp.mean(v, axis=-1, keepdims=True)
    d = v - mu
    var = jnp.mean(d * d, axis=-1, keepdims=True)
    return d * lax.rsqrt(var + EPS) * g + b


def _dot(a, b):
    return jnp.dot(a, b, preferred_element_type=F32)


def _dot_nt(a, b):
    return lax.dot_general(a, b, (((1,), (1,)), ((), ())), preferred_element_type=F32)


def _causal_conv(x_ref, tail_ref, cw_ref, cb_ref, rows):
    width = x_ref.shape[-1]
    groups = rows // SUBLANES
    x = x_ref[...].reshape(groups, SUBLANES, width)
    ext = jnp.concatenate([tail_ref[...][None], x], axis=0)
    sub = lax.broadcasted_iota(jnp.int32, (groups, SUBLANES, width), 1)
    out = cb_ref[...] + cw_ref[CONV_WIDTH - 1:CONV_WIDTH, :] * x
    for s in range(1, CONV_WIDTH):
        rot = pltpu.roll(ext, s, 1)
        shifted = jnp.where(sub >= s, rot[1:], rot[:-1])
        out = out + cw_ref[CONV_WIDTH - 1 - s:CONV_WIDTH - s, :] * shifted
    tail_ref[...] = x[groups - 1]
    return out.reshape(rows, width)


def _expand_heads(v, e):
    hi = v.astype(BF16)
    lo = (v - hi.astype(F32)).astype(BF16)
    return _dot(hi, e) + _dot(lo, e)


def _ssd_chunk(z_ref, xbc_ref, dt_ref, cw_ref, cb_ref, dtb_ref, alog_ref, dsk_ref, nw_ref, e_ref,
               xpad_ref, st_ref, tick):
    l = SSD_CHUNK
    gw = SSD_INNER // SSD_GROUPS
    tick()
    xbc = jax.nn.silu(_causal_conv(xbc_ref, xpad_ref, cw_ref, cb_ref, l))
    tick()
    xs = xbc[:, :SSD_INNER]
    bm = xbc[:, SSD_INNER:SSD_INNER + SSD_GROUPS * SSD_STATE]
    cm = xbc[:, SSD_INNER + SSD_GROUPS * SSD_STATE:]

    dt = jax.nn.softplus(dt_ref[...] + dtb_ref[...])
    adt = dt * (-jnp.exp(alog_ref[...]))
    row = lax.broadcasted_iota(jnp.int32, (l, l), 0)
    col = lax.broadcasted_iota(jnp.int32, (l, l), 1)
    causal = row >= col
    a_cum = jnp.dot(causal.astype(F32), adt, precision=lax.Precision.HIGHEST,
                    preferred_element_type=F32)
    a_cum_t = a_cum.T
    a_last = a_cum[l - 1:l, :]
    ea = jnp.exp(a_cum)
    stacked = jnp.concatenate([dt, ea, dt * jnp.exp(a_last - a_cum)], axis=0)
    ex = _expand_heads(stacked, e_ref[...])
    dt_x, ea_x, ds_x = ex[:l], ex[l:2 * l], ex[2 * l:]
    xdt = xs * dt_x
    x2 = (xs * ds_x).astype(BF16)
    cd_x = ea_x[l - 1:l, :]

    lane = lax.broadcasted_iota(jnp.int32, (l, LANES), 1)
    lo_half = lane < SSD_HEAD_DIM
    y_parts = []
    for g in range(SSD_GROUPS):
        bg = bm[:, g * SSD_STATE:(g + 1) * SSD_STATE]
        cg = cm[:, g * SSD_STATE:(g + 1) * SSD_STATE].astype(BF16)
        cb = _dot_nt(cg, bg.astype(BF16))
        heads_per_group = SSD_HEADS // SSD_GROUPS
        diag = []
        for pair in range(heads_per_group // 2):
            tick()
            blk = g * (heads_per_group // 2) + pair
            xp = xdt[:, blk * LANES:(blk + 1) * LANES]
            acc = None
            for half in range(2):
                h = 2 * blk + half
                seg = a_cum[:, h:h + 1] - a_cum_t[h:h + 1, :]
                dec = jnp.exp(jnp.where(causal, seg, -jnp.inf))
                m = (cb * dec).astype(BF16)
                keep = lo_half if half == 0 else jnp.logical_not(lo_half)
                part = _dot(m, jnp.where(keep, xp, 0.0).astype(BF16))
                acc = part if acc is None else acc + part
            diag.append(acc)
        y_diag = jnp.concatenate(diag, axis=1)
        prev = st_ref[g]
        y_off = _dot(cg, prev.astype(BF16)) * ea_x[:, g * gw:(g + 1) * gw]
        st_new = _dot(bg.T.astype(BF16), x2[:, g * gw:(g + 1) * gw])
        st_ref[g] = prev * cd_x[:, g * gw:(g + 1) * gw] + st_new
        y_parts.append(y_diag + y_off)

    tick()
    y = jnp.concatenate(y_parts, axis=1) + dsk_ref[...] * xs
    y = y * jax.nn.silu(z_ref[...])
    normed = []
    for g in range(SSD_GROUPS):
        yg = y[:, g * gw:(g + 1) * gw]
        normed.append(yg * lax.rsqrt(jnp.mean(yg * yg, axis=-1, keepdims=True) + EPS))
    return jnp.concatenate(normed, axis=1) * nw_ref[...]


def _lru_block(gate_ref, xl_ref, cw_ref, cb_ref, wa_ref, ba_ref, wx_ref, bx_ref, lam_ref,
               xpad_ref, h_ref, rows, tick):
    tick()
    xl = _causal_conv(xl_ref, xpad_ref, cw_ref, cb_ref, rows)
    tick()
    xlb = xl.astype(BF16)
    n_tiles = LRU_WIDTH // LRU_TILE

    def gate(w_ref, b_ref):
        parts = [_dot(xlb[:, t * LRU_TILE:(t + 1) * LRU_TILE], w_ref[t]) for t in range(n_tiles)]
        return jax.nn.sigmoid(jnp.concatenate(parts, axis=1) + b_ref[...])

    r = gate(wa_ref, ba_ref)
    tick()
    i = gate(wx_ref, bx_ref)
    tick()
    log_a = -LRU_C * r * jax.nn.softplus(-lam_ref[...])
    a = jnp.exp(log_a)
    mult = jnp.sqrt(jnp.maximum(1.0 - jnp.exp(2.0 * log_a), 0.0))
    tick()
    u = mult * (i * xl)

    groups = rows // SUBLANES
    a = a.reshape(groups, SUBLANES, LRU_WIDTH)
    u = u.reshape(groups, SUBLANES, LRU_WIDTH)
    sub = lax.broadcasted_iota(jnp.int32, (groups, SUBLANES, LRU_WIDTH), 1)
    k = 1
    while k < SUBLANES:
        valid = sub >= k
        a_sh = jnp.where(valid, pltpu.roll(a, k, 1), 1.0)
        u_sh = jnp.where(valid, pltpu.roll(u, k, 1), 0.0)
        u = a * u_sh + u
        a = a * a_sh
        k *= 2
        tick()
    carry = h_ref[...]
    hs = []
    for gi in range(groups):
        if gi % SUBLANES == 0:
            tick()
        hg = u[gi] + a[gi] * carry
        hs.append(hg)
        carry = jnp.broadcast_to(hg[SUBLANES - 1:SUBLANES, :], (SUBLANES, LRU_WIDTH))
    h_ref[...] = carry
    tick()
    h = jnp.concatenate(hs, axis=0)
    return h * jax.nn.gelu(gate_ref[...])


_Z0, _GATE0, _XL0, _XBC0, _DT0 = 0, D_MODEL, 2 * D_MODEL, 3 * D_MODEL, 3 * D_MODEL + SSD_CONV_DIM


def _mixer_kernel(x_ref, xn_ref, win_ref,
                  s_cw, s_cb, s_dtb, s_alog, s_dsk, s_nw, s_e,
                  l_cw, l_cb, l_wa, l_ba, l_wx, l_bx, l_lam,
                  wout_ref, g_ref, b_ref, o_ref,
                  pa_ref, pb_ref, s_xpad, s_st, l_xpad, l_h):
    r = MIX_ROWS
    first = jnp.logical_and(pl.program_id(0) == 0, pl.program_id(1) == 0)

    def projection_pieces(rows_ref, dst_ref):
        xb = rows_ref[...].astype(BF16)

        def piece(c0, width):
            dst_ref[:, c0:c0 + width] = _dot(xb, win_ref[:, c0:c0 + width])

        return [functools.partial(piece, c0, min(PROJ_PIECE, PROJ_COLS - c0))
                for c0 in range(0, PROJ_COLS, PROJ_PIECE)]

    def drain(pieces):
        while pieces:
            pieces.pop(0)()

    @pl.when(first)
    def _():
        drain(projection_pieces(x_ref.at[0:r], pa_ref))

    @pl.when(pl.program_id(1) == 0)
    def _():
        s_xpad[...] = jnp.zeros_like(s_xpad)
        s_st[...] = jnp.zeros_like(s_st)
        l_xpad[...] = jnp.zeros_like(l_xpad)
        l_h[...] = jnp.zeros_like(l_h)

    def mix(p_ref, row0, pieces):
        ticks = [0]

        def tick():
            ticks[0] += 1
            if pieces and ticks[0] % PROJ_TICKS_PER_PIECE == 0:
                pieces.pop(0)()

        ya = []
        for c0 in range(0, r, SSD_CHUNK):
            rows = pl.ds(c0, SSD_CHUNK)
            ya.append(_ssd_chunk(p_ref.at[rows, pl.ds(_Z0, SSD_INNER)],
                                 p_ref.at[rows, pl.ds(_XBC0, SSD_CONV_DIM)],
                                 p_ref.at[rows, pl.ds(_DT0, LANES)],
                                 s_cw, s_cb, s_dtb, s_alog, s_dsk, s_nw, s_e, s_xpad, s_st, tick))
        y_a = jnp.concatenate(ya, axis=0).astype(BF16)
        y_b = _lru_block(p_ref.at[:, pl.ds(_GATE0, LRU_WIDTH)], p_ref.at[:, pl.ds(_XL0, LRU_WIDTH)],
                         l_cw, l_cb, l_wa, l_ba, l_wx, l_bx, l_lam, l_xpad, l_h, r, tick).astype(BF16)
        drain(pieces)
        mixed = _dot(y_a, wout_ref[0:SSD_INNER, :]) + _dot(y_b, wout_ref[SSD_INNER:, :])
        o_ref[row0:row0 + r, :] = _layer_norm(ALPHA * x_ref[row0:row0 + r, :] + mixed,
                                              g_ref[...], b_ref[...])

    mix(pa_ref, 0, projection_pieces(x_ref.at[r:2 * r], pb_ref))
    mix(pb_ref, r, projection_pieces(xn_ref, pa_ref))


def _mixer_ln(x2d, bsz, seq, w_in, ssd_params, lru_params, w_out, g, b):
    r = MIX_ROWS
    t, d = x2d.shape
    steps = seq // (2 * r)
    last_blk = t // r - 1
    const = lambda a: pl.BlockSpec(a.shape, lambda bi, c: (0,) * a.ndim, pipeline_mode=pl.Buffered(1))
    params = [w_in, *ssd_params, *lru_params, w_out, g, b]
    return pl.pallas_call(
        _mixer_kernel,
        grid=(bsz, steps),
        in_specs=[pl.BlockSpec((2 * r, d), lambda bi, c: (bi * steps + c, 0)),
                  pl.BlockSpec((r, d), lambda bi, c: (jnp.minimum(2 * (bi * steps + c) + 2, last_blk), 0)),
                  *[const(a) for a in params]],
        out_specs=pl.BlockSpec((2 * r, d), lambda bi, c: (bi * steps + c, 0)),
        out_shape=jax.ShapeDtypeStruct((t, d), F32),
        scratch_shapes=[pltpu.VMEM((r, PROJ_COLS), F32), pltpu.VMEM((r, PROJ_COLS), F32),
                        pltpu.VMEM((SUBLANES, SSD_CONV_DIM), F32),
                        pltpu.VMEM((SSD_GROUPS, SSD_STATE, SSD_INNER // SSD_GROUPS), F32),
                        pltpu.VMEM((SUBLANES, LRU_WIDTH), F32),
                        pltpu.VMEM((SUBLANES, LRU_WIDTH), F32)],
        compiler_params=_cparams("arbitrary", "arbitrary"),
        name="mixer_ln",
    )(x2d, x2d, *params)


def _mlp_residual(x, w1_ref, w2_ref, after_first_chunk=None):
    xb = x.astype(BF16)
    acc = ALPHA * x
    for c0 in range(0, D_FF, MLP_FF_CHUNK):
        h = jnp.square(jnp.maximum(_dot(xb, w1_ref[:, c0:c0 + MLP_FF_CHUNK]), 0.0))
        acc = acc + _dot(h.astype(BF16), w2_ref[c0:c0 + MLP_FF_CHUNK, :])
        if c0 == 0 and after_first_chunk is not None:
            after_first_chunk()
    return acc


def _mlp_ln_kernel(x_ref, w1_ref, w2_ref, g_ref, b_ref, o_ref):
    half = x_ref.shape[0] // 2

    def finish(r0, acc):
        o_ref[r0:r0 + half, :] = _layer_norm(acc, g_ref[...], b_ref[...])

    acc_a = _mlp_residual(x_ref[0:half, :], w1_ref, w2_ref)
    acc_b = _mlp_residual(x_ref[half:, :], w1_ref, w2_ref,
                          after_first_chunk=functools.partial(finish, 0, acc_a))
    finish(half, acc_b)


def _mlp_ln(x2d, w1, w2, g, b, tm=1024):
    t, d = x2d.shape
    const = lambda a: pl.BlockSpec(a.shape, lambda i: (0,) * a.ndim, pipeline_mode=pl.Buffered(1))
    return pl.pallas_call(
        _mlp_ln_kernel,
        grid=(t // tm,),
        in_specs=[pl.BlockSpec((tm, d), lambda i: (i, 0)), const(w1), const(w2), const(g), const(b)],
        out_specs=pl.BlockSpec((tm, d), lambda i: (i, 0)),
        out_shape=jax.ShapeDtypeStruct((t, d), F32),
        compiler_params=_cparams("arbitrary"),
        name="mlp_ln",
    )(x2d, w1, w2, g, b)


def _attn_out_mlp_kernel(a_ref, res_ref, wo_ref, g1_ref, b1_ref, w1_ref, w2_ref, g2_ref, b2_ref, o_ref):
    half = a_ref.shape[0] // 2

    def project(r0):
        mixed = _dot(a_ref[r0:r0 + half, :], wo_ref[...])
        return _layer_norm(ALPHA * res_ref[r0:r0 + half, :] + mixed, g1_ref[...], b1_ref[...])

    def finish(r0, acc):
        o_ref[r0:r0 + half, :] = _layer_norm(acc, g2_ref[...], b2_ref[...])

    x1 = [project(0)]
    acc_a = _mlp_residual(x1[0], w1_ref, w2_ref, after_first_chunk=lambda: x1.append(project(half)))
    acc_b = _mlp_residual(x1[1], w1_ref, w2_ref,
                          after_first_chunk=functools.partial(finish, 0, acc_a))
    finish(half, acc_b)


def _attn_out_mlp(a, res, wo, g1, b1, w1, w2, g2, b2, tm=1024):
    t, d = res.shape
    const = lambda p: pl.BlockSpec(p.shape, lambda i: (0,) * p.ndim, pipeline_mode=pl.Buffered(1))
    rows = lambda width: pl.BlockSpec((tm, width), lambda i: (i, 0))
    return pl.pallas_call(
        _attn_out_mlp_kernel,
        grid=(t // tm,),
        in_specs=[rows(a.shape[1]), rows(d), *[const(p) for p in (wo, g1, b1, w1, w2, g2, b2)]],
        out_specs=rows(d),
        out_shape=jax.ShapeDtypeStruct((t, d), F32),
        compiler_params=_cparams("arbitrary"),
        name="attn_out_mlp",
    )(a, res, wo, g1, b1, w1, w2, g2, b2)


def _qkv_rope_kernel(x_ref, w_ref, pos_ref, freq_ref, s1_ref, s2_ref, qt_ref, k_ref, vt_ref):
    qk_w = DIFF_HEADS * DIFF_V_DIM
    acc = _dot(x_ref[...].astype(BF16), w_ref[...])
    ang = pos_ref[...].astype(F32) * freq_ref[...]
    cos, sin = jnp.cos(ang), jnp.sin(ang)
    reps = qk_w // LANES
    cos_w = jnp.concatenate([cos] * reps, axis=1)
    sin1_w = jnp.concatenate([sin * s1_ref[...]] * reps, axis=1)
    sin2_w = jnp.concatenate([sin * s2_ref[...]] * reps, axis=1)
    half = ROPE_DIM // 2

    def rope(t):
        return (t * cos_w + pltpu.roll(t, half, 1) * sin1_w
                + pltpu.roll(t, qk_w - half, 1) * sin2_w)

    rows = acc.shape[0]
    q = rope(acc[:, :qk_w]) * (DIFF_HEAD_DIM ** -0.5 * LOG2_E)
    qt_ref[...] = q.T.astype(qt_ref.dtype).reshape(DIFF_HEADS, DIFF_V_DIM, rows)
    k_ref[...] = rope(acc[:, qk_w:2 * qk_w]).astype(k_ref.dtype)
    vt_ref[...] = acc[:, 2 * qk_w:].T.astype(vt_ref.dtype).reshape(DIFF_HEADS, DIFF_V_DIM, rows)


def _qkv_rope(x2d, w, pos, freq, s1, s2, bsz, seq, blk):
    t, d = x2d.shape
    n = w.shape[1]
    nb = seq // blk
    vec = pl.BlockSpec((1, LANES), lambda i: (0, 0))
    tr_spec = pl.BlockSpec((None, DIFF_HEADS, None, DIFF_V_DIM, blk),
                           lambda i: (i // nb, 0, i % nb, 0, 0))
    tr_shape = jax.ShapeDtypeStruct((bsz, DIFF_HEADS, nb, DIFF_V_DIM, blk), BF16)
    return pl.pallas_call(
        _qkv_rope_kernel,
        grid=(t // blk,),
        in_specs=[pl.BlockSpec((blk, d), lambda i: (i, 0)),
                  pl.BlockSpec((d, n), lambda i: (0, 0)),
                  pl.BlockSpec((blk, 1), lambda i: (i, 0)),
                  vec, vec, vec],
        out_specs=[tr_spec, pl.BlockSpec((blk, n // 3), lambda i: (i, 0)), tr_spec],
        out_shape=[tr_shape, jax.ShapeDtypeStruct((t, n // 3), BF16), tr_shape],
        compiler_params=_cparams("arbitrary"),
        name="qkv_rope",
    )(x2d, w, pos, freq, s1, s2)


def _attn_kernel(qt_ref, k_ref, vt_ref, lq1_ref, lk1_ref, lq2_ref, lk2_ref, sw_ref, o_ref,
                 m_ref, acc_ref, sta_ref, stb_ref, *, blk, ratio, lambda_init):
    assert ratio % 2 == 0
    qb = ratio * blk
    qt = jnp.concatenate([qt_ref[r] for r in range(ratio)], axis=1).astype(F32)
    first = lax.broadcasted_iota(jnp.int32, (DIFF_V_DIM, qb), 0) < DIFF_HEAD_DIM
    qts = (jnp.where(first, qt, 0.0).astype(BF16), jnp.where(first, 0.0, qt).astype(BF16))

    m_ref[...] = jnp.full_like(m_ref, -jnp.inf)
    acc_ref[...] = jnp.zeros_like(acc_ref)
    ones_rows = jnp.ones((BF16_ROWS, blk), BF16)

    w = ATTN_Q_PIECE

    def scores(j, st_ref, c0, nk=blk):
        start = pl.multiple_of(j * blk, blk)
        k = k_ref[pl.ds(start, nk), :]
        for comp in range(2):
            st_ref[comp, :nk, c0:c0 + w] = _dot(k, qts[comp][:, c0:c0 + w])

    def accumulate(j, st_ref, c0, diag_off=None):
        nk = blk if diag_off is None else min(blk, diag_off + w)
        vt = jnp.concatenate([vt_ref[j], ones_rows], axis=0)[:, :nk]
        if diag_off is not None:
            key = lax.broadcasted_iota(jnp.int32, (nk, w), 0)
            qry = lax.broadcasted_iota(jnp.int32, (nk, w), 1) + diag_off
            keep = key <= qry
        for comp in range(2):
            st = st_ref[comp, :nk, c0:c0 + w]
            if diag_off is not None:
                st = jnp.where(keep, st, -jnp.inf)
            m_prev = m_ref[comp, :, c0:c0 + w]
            m_new = jnp.maximum(m_prev, jnp.max(st, axis=0, keepdims=True))
            alpha = jnp.exp2(m_prev - m_new)
            p = jnp.exp2(st - m_new)
            acc_ref[comp, :, c0:c0 + w] = (alpha * acc_ref[comp, :, c0:c0 + w]
                                           + _dot(vt, p.astype(BF16)))
            m_ref[comp, :, c0:c0 + w] = m_new

    n_full = ratio * pl.program_id(2)
    slots = (sta_ref, stb_ref)

    def pair(u, carry):
        for half in range(2):
            j = 2 * u + half
            for c0 in range(0, qb, w):
                scores(j + 1, slots[1 - half], c0)
                accumulate(j, slots[half], c0)
        return carry

    for c0 in range(0, qb, w):
        scores(0, sta_ref, c0)
    lax.fori_loop(0, n_full // 2, pair, 0)
    for s in range(ratio):
        for c0 in range(s * blk, qb, w):
            if s + 1 < ratio and c0 >= (s + 1) * blk:
                scores(n_full + s + 1, slots[(s + 1) % 2], c0, min(blk, c0 - (s + 1) * blk + w))
            accumulate(n_full + s, slots[s % 2], c0,
                       diag_off=c0 - s * blk if c0 < (s + 1) * blk else None)

    lam = (jnp.exp(jnp.sum(lq1_ref[...] * lk1_ref[...], axis=-1, keepdims=True))
           - jnp.exp(jnp.sum(lq2_ref[...] * lk2_ref[...], axis=-1, keepdims=True)) + lambda_init)
    hd = DIFF_V_DIM
    outs = [acc_ref[comp, :hd, :] / acc_ref[comp, hd:hd + 1, :] for comp in range(2)]
    ot = outs[0] - lam * outs[1]
    ot = ot * lax.rsqrt(jnp.mean(ot * ot, axis=0, keepdims=True) + EPS)
    ot = ot * sw_ref[...] * (1.0 - lambda_init)
    o_ref[...] = ot.T.astype(o_ref.dtype)


def _attn(qt, k, vt, lq1, lk1, lq2, lk2, sw_col, lambda_init, ratio=ATTN_Q_RATIO):
    bsz, heads, nb, hd, blk = qt.shape
    seq = nb * blk
    qb = ratio * blk
    nq = nb // ratio
    vec = lambda w: pl.BlockSpec((1, w), lambda b, h, i: (0, 0))
    return pl.pallas_call(
        functools.partial(_attn_kernel, blk=blk, ratio=ratio, lambda_init=lambda_init),
        grid=(bsz, heads, nq),
        in_specs=[pl.BlockSpec((None, None, ratio, hd, blk), lambda b, h, i: (b, h, i, 0, 0)),
                  pl.BlockSpec((seq, hd), lambda b, h, i: (b, h)),
                  pl.BlockSpec((None, None, nb, hd, blk), lambda b, h, i: (b, h, 0, 0, 0)),
                  vec(DIFF_HEAD_DIM), vec(DIFF_HEAD_DIM), vec(DIFF_HEAD_DIM), vec(DIFF_HEAD_DIM),
                  pl.BlockSpec((hd, 1), lambda b, h, i: (0, 0))],
        out_specs=pl.BlockSpec((qb, hd), lambda b, h, i: (b * nq + i, h)),
        out_shape=jax.ShapeDtypeStruct((bsz * seq, heads * hd), BF16),
        scratch_shapes=[pltpu.VMEM((2, 1, qb), F32),
                        pltpu.VMEM((2, hd + BF16_ROWS, qb), F32),
                        pltpu.VMEM((2, blk, qb), F32), pltpu.VMEM((2, blk, qb), F32)],
        compiler_params=_cparams("arbitrary", "arbitrary", "arbitrary"),
        name="diff_attn",
    )(qt, k, vt, lq1, lk1, lq2, lk2, sw_col)


def _block_diag_tiles(w):
    per = LRU_TILE // LRU_BLOCK
    w4 = w.reshape(LRU_HEADS // per, per, LRU_BLOCK, LRU_BLOCK)
    bd = jnp.einsum('tgij,gh->tgihj', w4, jnp.eye(per, dtype=w.dtype))
    return bd.reshape(LRU_HEADS // per, LRU_TILE, LRU_TILE)


def _pad_lanes(v):
    return jnp.pad(v, (0, LANES - v.shape[0]))[None, :]


def kernel(x, positions, ssm_w_in, ssm_conv_w, ssm_conv_b, ssm_dt_bias, ssm_a_log, ssm_d, ssm_norm_w, lru_conv_w, lru_conv_b, lru_w_a, lru_b_a, lru_w_x, lru_b_x, lru_lambda, mix_w_out, attn_w_qkv, attn_lq1, attn_lk1, attn_lq2, attn_lk2, attn_subln_w, attn_w_out, ln1_g, ln1_b, ff_w1, ff_w2, ln2_g, ln2_b):
    bsz, seq, d = x.shape
    h = x.reshape(bsz * seq, d)
    row = lambda v: v[None, :]

    head_of_lane = jnp.arange(SSD_INNER) // SSD_HEAD_DIM
    expand = (jnp.arange(LANES)[:, None] == head_of_lane[None, :]).astype(BF16)

    inv_freq = ROPE_THETA ** (-jnp.arange(0, ROPE_DIM, 2, dtype=F32) / ROPE_DIM)
    lane = jnp.arange(LANES)
    in_head = lane % DIFF_HEAD_DIM
    freq = jnp.where(in_head < ROPE_DIM, inv_freq[lane % (ROPE_DIM // 2)], 0.0)[None, :]
    sin_up = ((in_head >= ROPE_DIM // 2) & (in_head < ROPE_DIM)).astype(F32)[None, :]
    sin_dn = -(in_head < ROPE_DIM // 2).astype(F32)[None, :]
    pos = positions.reshape(bsz * seq, 1)

    for layer in range(DEPTH):
        i = layer // 2
        mlp_params = (ff_w1[layer].astype(BF16), ff_w2[layer].astype(BF16),
                      row(ln2_g[layer]), row(ln2_b[layer]))
        if layer % 2 == 0:
            w_in = ssm_w_in[i]
            s0, s1, s2, s3 = (SSD_INNER, SSD_INNER + SSD_CONV_DIM,
                              SSD_INNER + SSD_CONV_DIM + SSD_HEADS,
                              SSD_INNER + SSD_CONV_DIM + SSD_HEADS + LRU_WIDTH)
            w_bf = w_in.astype(BF16)
            w_perm = jnp.concatenate(
                [w_bf[:, :s0], w_bf[:, s2:s3], w_bf[:, s3:], w_bf[:, s0:s1], w_bf[:, s1:s2],
                 jnp.zeros((d, LANES - SSD_HEADS), BF16)], axis=1)
            ssd_params = (ssm_conv_w[i], row(ssm_conv_b[i]),
                          _pad_lanes(ssm_dt_bias[i]), _pad_lanes(ssm_a_log[i]),
                          row(jnp.repeat(ssm_d[i], SSD_HEAD_DIM)), row(ssm_norm_w[i]), expand)
            lru_params = (lru_conv_w[i], row(lru_conv_b[i]),
                          _block_diag_tiles(lru_w_a[i]).astype(BF16), row(lru_b_a[i].reshape(-1)),
                          _block_diag_tiles(lru_w_x[i]).astype(BF16), row(lru_b_x[i].reshape(-1)),
                          row(lru_lambda[i]))
            h = _mixer_ln(h, bsz, seq, w_perm, ssd_params, lru_params, mix_w_out[i].astype(BF16),
                          row(ln1_g[layer]), row(ln1_b[layer]))
            h = _mlp_ln(h, *mlp_params)
        else:
            lambda_init = 0.8 - 0.6 * math.exp(-0.3 * layer)
            qt, k, vt = _qkv_rope(h, attn_w_qkv[i].astype(BF16), pos, freq, sin_up, sin_dn,
                                  bsz, seq, ATTN_BLOCK)
            o = _attn(qt, k, vt, row(attn_lq1[i]), row(attn_lk1[i]), row(attn_lq2[i]),
                      row(attn_lk2[i]), attn_subln_w[i][:, None], lambda_init)
            h = _attn_out_mlp(o, h, attn_w_out[i].astype(BF16), row(ln1_g[layer]), row(ln1_b[layer]),
                              *mlp_params)
    return h.reshape(bsz, seq, d)
```
